```python
import math
import jax, jax.numpy as jnp
from jax import lax
import numpy as np

D_MODEL = 2048
BATCH = 8
SEQ = 2048
DEPTH = 2

HEAD_DIM = 128
N_HEADS = D_MODEL // HEAD_DIM
N_SB_HEADS = N_HEADS // 2
N_FOX_HEADS = N_HEADS - N_SB_HEADS
D_FF = 256 * ((8 * D_MODEL // 3 + 255) // 256)
Q_BLOCK = 128
ROPE_THETA = 500000.0
ROPE_DIMS = HEAD_DIM // 4
DILATED_PATTERNS = ((128, 1), (512, 4), (2048, 16))
RMS_EPS = 1e-6
NEG_INF = -1e30
N_EVEN = (DEPTH + 1) // 2
N_ODD = DEPTH // 2

kernel_name = "hybrid_sb_fox_dilated_macaron"


def rmsnorm(x, g):
    xf = x.astype(jnp.float32)
    y = xf * lax.rsqrt(jnp.mean(xf * xf, axis=-1, keepdims=True) + RMS_EPS)
    return (y * g.astype(jnp.float32)).astype(x.dtype)


def swiglu(x, w_gate, w_up, w_down):
    return (jax.nn.silu(x @ w_gate) * (x @ w_up)) @ w_down


def partial_rope(x, positions):
    half = ROPE_DIMS // 2
    freqs = ROPE_THETA ** (-jnp.arange(half, dtype=jnp.float32) / half)
    ang = positions[:, None] * freqs[None, :]
    cos = jnp.cos(ang)[None, :, None, :]
    sin = jnp.sin(ang)[None, :, None, :]
    xr = x[..., :ROPE_DIMS].astype(jnp.float32)
    x1, x2 = xr[..., :half], xr[..., half:]
    rot = jnp.concatenate([x1 * cos - x2 * sin, x2 * cos + x1 * sin], axis=-1)
    return jnp.concatenate([rot.astype(x.dtype), x[..., ROPE_DIMS:]], axis=-1)


def stick_breaking_attention(q, k, v):
    B, H, S, hd = q.shape
    n_blocks = S // Q_BLOCK
    scale = hd ** -0.5
    key_pos = jnp.arange(S)

    def block(i):
        qb = lax.dynamic_slice_in_dim(q, i * Q_BLOCK, Q_BLOCK, axis=2)
        z = jnp.einsum('bhqd,bhkd->bhqk', qb, k).astype(jnp.float32) * scale
        q_pos = i * Q_BLOCK + jnp.arange(Q_BLOCK)
        strict = key_pos[None, :] < q_pos[:, None]
        log_not_beta = jnp.where(strict, -jax.nn.softplus(z), 0.0)
        after = lax.cumsum(log_not_beta, axis=3, reverse=True) - log_not_beta
        w = jnp.where(strict, jnp.exp(jax.nn.log_sigmoid(z) + after), 0.0)
        return jnp.einsum('bhqk,bhkd->bhqd', w.astype(v.dtype), v)

    out = lax.map(block, jnp.arange(n_blocks))
    return jnp.moveaxis(out, 0, 2).reshape(B, H, S, hd)


def forgetting_attention(q, k, v, log_f):
    B, H, S, hd = q.shape
    n_blocks = S // Q_BLOCK
    scale = hd ** -0.5
    cum_f = lax.cumsum(log_f, axis=2)
    key_pos = jnp.arange(S)

    def block(i):
        qb = lax.dynamic_slice_in_dim(q, i * Q_BLOCK, Q_BLOCK, axis=2)
        fq = lax.dynamic_slice_in_dim(cum_f, i * Q_BLOCK, Q_BLOCK, axis=2)
        z = (jnp.einsum('bhqd,bhkd->bhqk', qb, k).astype(jnp.float32) * scale
             + fq[..., :, None] - cum_f[:, :, None, :])
        q_pos = i * Q_BLOCK + jnp.arange(Q_BLOCK)
        causal = key_pos[None, :] <= q_pos[:, None]
        p = jax.nn.softmax(jnp.where(causal, z, NEG_INF), axis=-1)
        return jnp.einsum('bhqk,bhkd->bhqd', p.astype(v.dtype), v)

    out = lax.map(block, jnp.arange(n_blocks))
    return jnp.moveaxis(out, 0, 2).reshape(B, H, S, hd)


def dilated_branch(q, k, v, window, dilation):
    B, S, H, hd = q.shape
    steps = window // dilation
    L = S // dilation
    n_blocks = -(-L // steps)
    Lp = n_blocks * steps
    scale = hd ** -0.5

    def to_sub(t):
        return t.reshape(B, L, dilation, H, hd).transpose(0, 2, 3, 1, 4)

    qs = jnp.pad(to_sub(q), ((0, 0), (0, 0), (0, 0), (0, Lp - L), (0, 0)))
    qs = qs.reshape(B, dilation, H, n_blocks, steps, hd)

    def banded(t):
        tp = jnp.pad(to_sub(t), ((0, 0), (0, 0), (0, 0), (steps, Lp - L), (0, 0)))
        prev = tp[:, :, :, :Lp].reshape(B, dilation, H, n_blocks, steps, hd)
        cur = tp[:, :, :, steps:].reshape(B, dilation, H, n_blocks, steps, hd)
        return jnp.concatenate([prev, cur], axis=4)

    kb, vb = banded(k), banded(v)
    z = jnp.einsum('brhnqd,brhnkd->brhnqk', qs, kb).astype(jnp.float32) * scale
    a = jnp.arange(steps)[:, None]
    c = jnp.arange(2 * steps)[None, :]
    dist = a + steps - c
    key_sub = jnp.arange(n_blocks)[:, None, None] * steps + c - steps
    valid = (dist >= 0) & (dist <= steps) & (key_sub >= 0)
    z = jnp.where(valid, z, NEG_INF)
    m = jnp.max(z, axis=-1, keepdims=True)
    e = jnp.exp(z - m)
    denom = jnp.sum(e, axis=-1, keepdims=True)
    o = jnp.einsum('brhnqk,brhnkd->brhnqd', (e / denom).astype(v.dtype), vb)
    lse = (m + jnp.log(denom))[..., 0]
    o = o.reshape(B, dilation, H, Lp, hd)[:, :, :, :L]
    o = o.transpose(0, 3, 1, 2, 4).reshape(B, S, H, hd)
    lse = lse.reshape(B, dilation, H, Lp)[..., :L].transpose(0, 3, 1, 2).reshape(B, S, H)
    return o, lse


def dilated_attention(q, k, v):
    outs, lses = [], []
    for window, dilation in DILATED_PATTERNS:
        o, lse = dilated_branch(q, k, v, window, dilation)
        outs.append(o)
        lses.append(lse)
    w = jax.nn.softmax(jnp.stack(lses, axis=0), axis=0)
    out = jnp.sum(w[..., None] * jnp.stack(outs, axis=0).astype(jnp.float32), axis=0)
    return out.astype(q.dtype)


def sb_fox_mixer(x, w_in, b_forget, w_out):
    B, S, _ = x.shape
    h = x @ w_in
    qkv = h[..., :3 * D_MODEL].reshape(B, S, 3, N_HEADS, HEAD_DIM).transpose(2, 0, 3, 1, 4)
    q, k, v = qkv[0], qkv[1], qkv[2]
    log_f = jax.nn.log_sigmoid(
        (h[..., 3 * D_MODEL:] + b_forget).astype(jnp.float32)).transpose(0, 2, 1)
    o_sb = stick_breaking_attention(q[:, :N_SB_HEADS], k[:, :N_SB_HEADS], v[:, :N_SB_HEADS])
    o_fox = forgetting_attention(q[:, N_SB_HEADS:], k[:, N_SB_HEADS:], v[:, N_SB_HEADS:], log_f)
    o = jnp.concatenate([o_sb, o_fox], axis=1).transpose(0, 2, 1, 3).reshape(B, S, D_MODEL)
    return o @ w_out


def dilated_mixer(x, w_qkv, w_out):
    B, S, _ = x.shape
    qkv = (x @ w_qkv).reshape(B, S, 3, N_HEADS, HEAD_DIM)
    positions = jnp.arange(S, dtype=jnp.float32)
    q = partial_rope(qkv[:, :, 0], positions)
    k = partial_rope(qkv[:, :, 1], positions)
    v = qkv[:, :, 2]
    o = dilated_attention(q, k, v).reshape(B, S, D_MODEL)
    return o @ w_out


def setup_inputs(seed: int = 0) -> dict:
    key = jax.random.key(seed)
    ks = jax.random.split(key, 16)
    f32 = jnp.float32

    def dense(k, shape, fan_in):
        return jax.random.normal(k, shape, f32) * fan_in ** -0.5

    x = jax.random.normal(ks[0], (BATCH, SEQ, D_MODEL), f32)
    norm_g = 1.0 + 0.02 * jax.random.normal(ks[1], (DEPTH, 3, D_MODEL), f32)
    ffn1_w_gate = dense(ks[2], (DEPTH, D_MODEL, D_FF), D_MODEL)
    ffn1_w_up = dense(ks[3], (DEPTH, D_MODEL, D_FF), D_MODEL)
    ffn1_w_down = dense(ks[4], (DEPTH, D_FF, D_MODEL), D_FF)
    ffn2_w_gate = dense(ks[5], (DEPTH, D_MODEL, D_FF), D_MODEL)
    ffn2_w_up = dense(ks[6], (DEPTH, D_MODEL, D_FF), D_MODEL)
    ffn2_w_down = dense(ks[7], (DEPTH, D_FF, D_MODEL), D_FF)
    even_w_in = dense(ks[8], (N_EVEN, D_MODEL, 3 * D_MODEL + N_FOX_HEADS), D_MODEL)
    even_b_forget = 3.0 + 0.5 * jax.random.normal(ks[9], (N_EVEN, N_FOX_HEADS), f32)
    even_w_out = dense(ks[10], (N_EVEN, D_MODEL, D_MODEL), D_MODEL)
    odd_w_qkv = dense(ks[11], (N_ODD, D_MODEL, 3 * D_MODEL), D_MODEL)
    odd_w_out = dense(ks[12], (N_ODD, D_MODEL, D_MODEL), D_MODEL)
    final_norm_g = 1.0 + 0.02 * jax.random.normal(ks[13], (D_MODEL,), f32)
    return {
        "x": x, "norm_g": norm_g,
        "ffn1_w_gate": ffn1_w_gate, "ffn1_w_up": ffn1_w_up, "ffn1_w_down": ffn1_w_down,
        "ffn2_w_gate": ffn2_w_gate, "ffn2_w_up": ffn2_w_up, "ffn2_w_down": ffn2_w_down,
        "even_w_in": even_w_in, "even_b_forget": even_b_forget, "even_w_out": even_w_out,
        "odd_w_qkv": odd_w_qkv, "odd_w_out": odd_w_out, "final_norm_g": final_norm_g,
    }


def reference(x, norm_g, ffn1_w_gate, ffn1_w_up, ffn1_w_down, ffn2_w_gate, ffn2_w_up,
              ffn2_w_down, even_w_in, even_b_forget, even_w_out, odd_w_qkv, odd_w_out,
              final_norm_g):
    for layer in range(DEPTH):
        j = layer // 2
        x = x + 0.5 * swiglu(rmsnorm(x, norm_g[layer, 0]),
                             ffn1_w_gate[layer], ffn1_w_up[layer], ffn1_w_down[layer])
        h = rmsnorm(x, norm_g[layer, 1])
        if layer % 2 == 0:
            x = x + sb_fox_mixer(h, even_w_in[j], even_b_forget[j], even_w_out[j])
        else:
            x = x + dilated_mixer(h, odd_w_qkv[j], odd_w_out[j])
        x = x + 0.5 * swiglu(rmsnorm(x, norm_g[layer, 2]),
                             ffn2_w_gate[layer], ffn2_w_up[layer], ffn2_w_down[layer])
    return rmsnorm(x, final_norm_g)
```

```python
import functools

import jax
import jax.numpy as jnp
from jax import lax
from jax.experimental import pallas as pl
from jax.experimental.pallas import tpu as pltpu

_F32 = jnp.float32
_BF16 = jnp.bfloat16

HEAD_DIM = 128
N_SB_HEADS = 8
N_FOX_HEADS = 8
ROPE_DIMS = HEAD_DIM // 4
ROPE_THETA = 500000.0
RMS_EPS = 1e-6
NEG_INF = -1e30
DIL_W1, DIL_W2, DIL_D2, DIL_D3 = 128, 512, 4, 16

V7X_LANES = 128
V7X_VMEM_BYTES = 64 * 1024 * 1024
VMEM_CAP_BYTES = 56 * 1024 * 1024

ATT_BLOCK = 256


def _vmem_limit(block_bytes):
    return int(min(VMEM_CAP_BYTES, max(32 * 1024 * 1024, block_bytes * 5 // 4)))


def _rms(x, g):
    ms = jnp.mean(x * x, axis=-1, keepdims=True)
    return x * lax.rsqrt(ms + RMS_EPS) * g


def _log1p_exp_neg_abs(z):
    return jnp.log(1.0 + jnp.exp(-jnp.abs(z)))


def _ffn_kernel(*refs, final_norm, n_f):
    if final_norm:
        x_ref, g_ref, wg_ref, wu_ref, wd_ref, gf_ref, o_ref, n_ref = refs
    else:
        x_ref, g_ref, wg_ref, wu_ref, wd_ref, o_ref, n_ref = refs
    j = pl.program_id(1)

    @pl.when(j == 0)
    def _():
        x = x_ref[...]
        n_ref[...] = _rms(x, g_ref[...]).astype(_BF16)
        o_ref[...] = x

    n = n_ref[...]
    gate = jnp.dot(n, wg_ref[...], preferred_element_type=_F32)
    up = jnp.dot(n, wu_ref[...], preferred_element_type=_F32)
    h = (gate * jax.nn.sigmoid(gate)) * (up * 0.5)
    o_ref[...] += jnp.dot(h.astype(_BF16), wd_ref[...], preferred_element_type=_F32)

    if final_norm:
        @pl.when(j == n_f - 1)
        def _():
            o_ref[...] = _rms(o_ref[...], gf_ref[...])


def _ffn(x, g, wg, wu, wd, final_g=None, *, tm=512, tf=512):
    m, d = x.shape
    f = wg.shape[1]
    assert m % tm == 0 and f % tf == 0
    final_norm = final_g is not None
    in_specs = [
        pl.BlockSpec((tm, d), lambda i, j: (i, 0)),
        pl.BlockSpec((1, d), lambda i, j: (0, 0)),
        pl.BlockSpec((d, tf), lambda i, j: (0, j)),
        pl.BlockSpec((d, tf), lambda i, j: (0, j)),
        pl.BlockSpec((tf, d), lambda i, j: (j, 0)),
    ]
    args = [x, g.reshape(1, d), wg, wu, wd]
    if final_norm:
        in_specs.append(pl.BlockSpec((1, d), lambda i, j: (0, 0)))
        args.append(final_g.reshape(1, d))
    est = 2 * (2 * tm * d * 4) + tm * d * 2 + 2 * 3 * d * tf * 2 + 4 * tm * tf * 4
    return pl.pallas_call(
        functools.partial(_ffn_kernel, final_norm=final_norm, n_f=f // tf),
        grid=(m // tm, f // tf),
        in_specs=in_specs,
        out_specs=pl.BlockSpec((tm, d), lambda i, j: (i, 0)),
        out_shape=jax.ShapeDtypeStruct((m, d), _F32),
        scratch_shapes=[pltpu.VMEM((tm, d), _BF16)],
        compiler_params=pltpu.CompilerParams(
            dimension_semantics=("parallel", "arbitrary"),
            vmem_limit_bytes=_vmem_limit(est)),
        name="ffn_final" if final_norm else "ffn",
    )(*args)


def _proj_gate_kernel(x_ref, g_ref, w_ref, wf_ref, o_ref, gate_ref, n_ref):
    j = pl.program_id(1)

    @pl.when(j == 0)
    def _():
        n = _rms(x_ref[...], g_ref[...]).astype(_BF16)
        n_ref[...] = n
        gate_ref[...] = jnp.dot(n, wf_ref[...], preferred_element_type=_F32)

    o_ref[...] = jnp.dot(n_ref[...], w_ref[...],
                         preferred_element_type=_F32).astype(o_ref.dtype)


def _proj_rope_kernel(x_ref, g_ref, w_ref, cos_ref, sin_lo_ref, sin_hi_ref, o_ref, n_ref,
                      *, rope_blocks):
    j = pl.program_id(1)

    @pl.when(j == 0)
    def _():
        n_ref[...] = _rms(x_ref[...], g_ref[...]).astype(_BF16)

    y = jnp.dot(n_ref[...], w_ref[...], preferred_element_type=_F32)
    half = ROPE_DIMS // 2
    tn = y.shape[1]

    @pl.when(j < rope_blocks)
    def _():
        from_hi = pltpu.roll(y, tn - half, 1)
        from_lo = pltpu.roll(y, half, 1)
        r = y * cos_ref[...] + from_hi * sin_lo_ref[...] + from_lo * sin_hi_ref[...]
        o_ref[...] = r.astype(o_ref.dtype)

    @pl.when(j >= rope_blocks)
    def _():
        o_ref[...] = y.astype(o_ref.dtype)


def _rope_tables(seq, tn):
    half = ROPE_DIMS // 2
    freqs = ROPE_THETA ** (-jnp.arange(half, dtype=_F32) / half)
    ang = jnp.arange(seq, dtype=_F32)[:, None] * freqs[None, :]
    cos, sin = jnp.cos(ang), jnp.sin(ang)
    pad = jnp.zeros((seq, HEAD_DIM - ROPE_DIMS), _F32)
    zero = jnp.zeros((seq, half), _F32)
    cos_t = jnp.concatenate([cos, cos, pad + 1.0], axis=1)
    sin_lo = jnp.concatenate([-sin, zero, pad], axis=1)
    sin_hi = jnp.concatenate([zero, sin, pad], axis=1)
    reps = tn // HEAD_DIM
    return tuple(jnp.tile(t, (1, reps)) for t in (cos_t, sin_lo, sin_hi))


def _proj(x, g, w, *, w_gate=None, rope_seq=None, rope_cols=0, tm=1024, tn=512):
    m, d = x.shape
    n_out = w.shape[1]
    assert m % tm == 0 and n_out % tn == 0
    in_specs = [
        pl.BlockSpec((tm, d), lambda i, j: (i, 0)),
        pl.BlockSpec((1, d), lambda i, j: (0, 0)),
        pl.BlockSpec((d, tn), lambda i, j: (0, j)),
    ]
    args = [x, g.reshape(1, d), w]
    out_specs = pl.BlockSpec((tm, tn), lambda i, j: (i, j))
    out_shape = jax.ShapeDtypeStruct((m, n_out), _BF16)
    est = 2 * tm * d * 4 + tm * d * 2 + 2 * d * tn * 2 + 2 * tm * tn * 2 + 2 * tm * tn * 4
    if w_gate is not None:
        ng = w_gate.shape[1]
        in_specs.append(pl.BlockSpec((d, ng), lambda i, j: (0, 0)))
        args.append(w_gate)
        out_specs = (out_specs, pl.BlockSpec((tm, ng), lambda i, j: (i, 0)))
        out_shape = (out_shape, jax.ShapeDtypeStruct((m, ng), _F32))
        body = _proj_gate_kernel
        name = "proj_gate"
    else:
        assert rope_seq % tm == 0 and rope_cols % tn == 0
        seq_blocks = rope_seq // tm
        tab_spec = pl.BlockSpec((tm, tn), lambda i, j: (i % seq_blocks, 0))
        in_specs += [tab_spec] * 3
        args += list(_rope_tables(rope_seq, tn))
        est += 3 * 2 * tm * tn * 4
        body = functools.partial(_proj_rope_kernel, rope_blocks=rope_cols // tn)
        name = "proj_rope"
    return pl.pallas_call(
        body,
        grid=(m // tm, n_out // tn),
        in_specs=in_specs,
        out_specs=out_specs,
        out_shape=out_shape,
        scratch_shapes=[pltpu.VMEM((tm, d), _BF16)],
        compiler_params=pltpu.CompilerParams(
            dimension_semantics=("parallel", "arbitrary"),
            vmem_limit_bytes=_vmem_limit(est)),
        name=name,
    )(*args)


def _split3(x):
    hi = x.astype(_BF16)
    r = x - hi.astype(_F32)
    mid = r.astype(_BF16)
    lo = (r - mid.astype(_F32)).astype(_BF16)
    return hi, mid, lo


def _gates_kernel(gp_ref, b_ref, col_ref, row_ref, *, blk):
    seq = gp_ref.shape[1]
    z = gp_ref[0] + b_ref[...]
    log_f = jnp.minimum(z, 0.0) - _log1p_exp_neg_abs(z)
    r = lax.broadcasted_iota(jnp.int32, (blk, blk), 0)
    c = lax.broadcasted_iota(jnp.int32, (blk, blk), 1)
    tri = jnp.where(c <= r, 1.0, 0.0).astype(_BF16)
    carry = jnp.zeros((1, log_f.shape[1]), _F32)
    for s in range(seq // blk):
        part = log_f[s * blk:(s + 1) * blk]
        cs = carry
        for p in _split3(part):
            cs = cs + jnp.dot(tri, p, preferred_element_type=_F32)
        col_ref[0, s * blk:(s + 1) * blk, :] = cs
        carry = cs[blk - 1:blk, :]
    row_ref[0] = jnp.transpose(col_ref[0])[:row_ref.shape[1], :]


def _gates(gate_pre, b_forget, batch, seq):
    ng = gate_pre.shape[1]
    nh = b_forget.shape[0]
    b_pad = jnp.zeros((1, ng), _F32).at[0, :nh].set(b_forget)
    return pl.pallas_call(
        functools.partial(_gates_kernel, blk=ATT_BLOCK),
        grid=(batch,),
        in_specs=[pl.BlockSpec((1, seq, ng), lambda b: (b, 0, 0)),
                  pl.BlockSpec((1, ng), lambda b: (0, 0))],
        out_specs=(pl.BlockSpec((1, seq, ng), lambda b: (b, 0, 0)),
                   pl.BlockSpec((1, nh, seq), lambda b: (b, 0, 0))),
        out_shape=(jax.ShapeDtypeStruct((batch, seq, ng), _F32),
                   jax.ShapeDtypeStruct((batch, nh, seq), _F32)),
        compiler_params=pltpu.CompilerParams(dimension_semantics=("parallel",)),
        name="gates",
    )(gate_pre.reshape(batch, seq, ng), b_pad)


def _qk(q, k):
    return lax.dot_general(q, k, (((1,), (1,)), ((), ())), preferred_element_type=_F32)


def _sb_head(q_ref, k_ref, v_ref, o_ref, i, blk):
    scale = HEAD_DIM ** -0.5
    q = q_ref[...]
    r = lax.broadcasted_iota(jnp.int32, (blk, blk), 0)
    c = lax.broadcasted_iota(jnp.int32, (blk, blk), 1)
    strict = c < r
    suffix = jnp.where(r > c, 1.0, 0.0).astype(_BF16)

    def block(j, carry, acc, masked):
        start = pl.multiple_of(j * blk, blk)
        k = k_ref[pl.ds(start, blk), :]
        v = v_ref[pl.ds(start, blk), :]
        z = _qk(q, k) * scale
        l1p = _log1p_exp_neg_abs(z)
        log_not_beta = -jnp.maximum(z, 0.0) - l1p
        log_beta = jnp.minimum(z, 0.0) - l1p
        if masked:
            log_not_beta = jnp.where(strict, log_not_beta, 0.0)
        hi = log_not_beta.astype(_BF16)
        lo = (log_not_beta - hi.astype(_F32)).astype(_BF16)
        after = (jnp.dot(hi, suffix, preferred_element_type=_F32)
                 + jnp.dot(lo, suffix, preferred_element_type=_F32) + carry)
        w = jnp.exp(log_beta + after)
        if masked:
            w = jnp.where(strict, w, 0.0)
        acc = acc + jnp.dot(w.astype(_BF16), v, preferred_element_type=_F32)
        carry = carry + jnp.sum(log_not_beta, axis=1, keepdims=True)
        return carry, acc

    carry, acc = block(i, jnp.zeros((blk, 1), _F32), jnp.zeros((blk, HEAD_DIM), _F32), True)

    def body(jj, state):
        return block(i - 1 - jj, state[0], state[1], False)

    carry, acc = lax.fori_loop(0, i, body, (carry, acc))
    o_ref[...] = acc.astype(o_ref.dtype)


def _softmax_block(z, v, m, l, acc, weight=None):
    m_new = jnp.maximum(m, jnp.max(z, axis=1, keepdims=True))
    alpha = jnp.exp(m - m_new)
    p = jnp.exp(z - m_new)
    if weight is not None:
        p = p * weight
    l = alpha * l + jnp.sum(p, axis=1, keepdims=True)
    acc = alpha * acc + jnp.dot(p.astype(_BF16), v, preferred_element_type=_F32)
    return m_new, l, acc


def _fox_head(q_ref, k_ref, v_ref, cfc_ref, cfr_ref, o_ref, hh, i, blk):
    scale = HEAD_DIM ** -0.5
    q = q_ref[...]
    r = lax.broadcasted_iota(jnp.int32, (blk, blk), 0)
    c = lax.broadcasted_iota(jnp.int32, (blk, blk), 1)
    causal = c <= r
    lane = lax.broadcasted_iota(jnp.int32, cfc_ref.shape[1:], 1)
    cf_q = jnp.sum(jnp.where(lane == hh, cfc_ref[0], 0.0), axis=1, keepdims=True)

    def block(j, m, l, acc, masked):
        start = pl.multiple_of(j * blk, blk)
        k = k_ref[pl.ds(start, blk), :]
        v = v_ref[pl.ds(start, blk), :]
        cf_k = cfr_ref[0, hh, pl.ds(j, 1), :]
        z = _qk(q, k) * scale + cf_q - cf_k
        if masked:
            z = jnp.where(causal, z, NEG_INF)
        return _softmax_block(z, v, m, l, acc)

    init = (jnp.full((blk, 1), NEG_INF, _F32), jnp.zeros((blk, 1), _F32),
            jnp.zeros((blk, HEAD_DIM), _F32))
    state = block(i, *init, True)

    def body(jj, state):
        return block(i - 1 - jj, *state, False)

    m, l, acc = lax.fori_loop(0, i, body, state)
    o_ref[...] = (acc / l).astype(o_ref.dtype)


def _attn_even_kernel(q_ref, k_ref, v_ref, cfc_ref, cfr_ref, o_ref, *, blk):
    h = pl.program_id(1)
    i = pl.program_id(2)

    @pl.when(h < N_SB_HEADS)
    def _():
        _sb_head(q_ref, k_ref, v_ref, o_ref, i, blk)

    @pl.when(h >= N_SB_HEADS)
    def _():
        _fox_head(q_ref, k_ref, v_ref, cfc_ref, cfr_ref, o_ref, h - N_SB_HEADS, i, blk)


def _attn_specs(batch, seq, n_heads, blk):
    nq = seq // blk
    q_spec = pl.BlockSpec((blk, HEAD_DIM), lambda b, h, i: (b * nq + i, h))
    k_spec = pl.BlockSpec((seq, HEAD_DIM), lambda b, h, i: (b, n_heads + h))
    v_spec = pl.BlockSpec((seq, HEAD_DIM), lambda b, h, i: (b, 2 * n_heads + h))
    o_spec = pl.BlockSpec((blk, HEAD_DIM), lambda b, h, i: (b * nq + i, h))
    return nq, q_spec, k_spec, v_spec, o_spec


def _attn_even(qkv, cf_col, cf_row, batch, seq, *, blk=ATT_BLOCK):
    n_heads = N_SB_HEADS + N_FOX_HEADS
    nq, q_spec, k_spec, v_spec, o_spec = _attn_specs(batch, seq, n_heads, blk)
    ng = cf_col.shape[2]
    cf_row = cf_row.reshape(batch, N_FOX_HEADS, nq, blk)
    return pl.pallas_call(
        functools.partial(_attn_even_kernel, blk=blk),
        grid=(batch, n_heads, nq),
        in_specs=[q_spec, k_spec, v_spec,
                  pl.BlockSpec((1, blk, ng), lambda b, h, i: (b, i, 0)),
                  pl.BlockSpec((1, N_FOX_HEADS, nq, blk), lambda b, h, i: (b, 0, 0, 0))],
        out_specs=o_spec,
        out_shape=jax.ShapeDtypeStruct((batch * seq, n_heads * HEAD_DIM), _BF16),
        compiler_params=pltpu.CompilerParams(
            dimension_semantics=("parallel", "parallel", "arbitrary")),
        name="attn_even",
    )(qkv, qkv, qkv, cf_col, cf_row)


def _pattern_count(delta):
    one, zero = jnp.float32(1.0), jnp.float32(0.0)
    c1 = jnp.where(delta <= DIL_W1, one, zero)
    c2 = jnp.where(((delta & (DIL_D2 - 1)) == 0) & (delta <= DIL_W2), one, zero)
    c3 = jnp.where((delta & (DIL_D3 - 1)) == 0, one, zero)
    return jnp.where(delta >= 0, c1 + c2 + c3, zero)


def _attn_dilated_kernel(q_ref, k_ref, v_ref, o_ref, *, blk):
    i = pl.program_id(2)
    scale = HEAD_DIM ** -0.5
    q = q_ref[...]
    r = lax.broadcasted_iota(jnp.int32, (blk, blk), 0)
    c = lax.broadcasted_iota(jnp.int32, (blk, blk), 1)
    diff = r - c
    n_near = (DIL_W2 + blk - 1) // blk + 1
    assert blk % DIL_D3 == 0 and n_near * blk - (blk - 1) > DIL_W2
    far_bias = jnp.where((diff & (DIL_D3 - 1)) == 0, 0.0, NEG_INF)

    def load(j):
        start = pl.multiple_of(j * blk, blk)
        return k_ref[pl.ds(start, blk), :], v_ref[pl.ds(start, blk), :]

    def near(d, state):
        k, v = load(i - d)
        cnt = _pattern_count(diff + d * blk)
        z = jnp.where(cnt > 0.0, _qk(q, k) * scale, NEG_INF)
        return _softmax_block(z, v, *state, weight=cnt)

    def far(jj, state):
        k, v = load(i - n_near - jj)
        z = _qk(q, k) * scale + far_bias
        return _softmax_block(z, v, *state)

    state = (jnp.full((blk, 1), NEG_INF, _F32), jnp.zeros((blk, 1), _F32),
             jnp.zeros((blk, HEAD_DIM), _F32))
    state = lax.fori_loop(0, jnp.minimum(i + 1, n_near), near, state)
    m, l, acc = lax.fori_loop(0, jnp.maximum(i + 1 - n_near, 0), far, state)
    o_ref[...] = (acc / l).astype(o_ref.dtype)


def _attn_dilated(qkv, batch, seq, n_heads, *, blk=ATT_BLOCK):
    assert seq <= DIL_D3 * DIL_W1
    nq, q_spec, k_spec, v_spec, o_spec = _attn_specs(batch, seq, n_heads, blk)
    return pl.pallas_call(
        functools.partial(_attn_dilated_kernel, blk=blk),
        grid=(batch, n_heads, nq),
        in_specs=[q_spec, k_spec, v_spec],
        out_specs=o_spec,
        out_shape=jax.ShapeDtypeStruct((batch * seq, n_heads * HEAD_DIM), _BF16),
        compiler_params=pltpu.CompilerParams(
            dimension_semantics=("parallel", "parallel", "arbitrary")),
        name="attn_dilated",
    )(qkv, qkv, qkv)


def _outproj_kernel(o_ref, w_ref, x_ref, y_ref):
    y_ref[...] = x_ref[...] + jnp.dot(o_ref[...], w_ref[...], preferred_element_type=_F32)


def _outproj(o, w, x, *, tm=1024, tn=512):
    m, d_in = o.shape
    d = w.shape[1]
    est = 2 * tm * d_in * 2 + 2 * d_in * tn * 2 + 4 * tm * tn * 4
    return pl.pallas_call(
        _outproj_kernel,
        grid=(m // tm, d // tn),
        in_specs=[pl.BlockSpec((tm, d_in), lambda i, j: (i, 0)),
                  pl.BlockSpec((d_in, tn), lambda i, j: (0, j)),
                  pl.BlockSpec((tm, tn), lambda i, j: (i, j))],
        out_specs=pl.BlockSpec((tm, tn), lambda i, j: (i, j)),
        out_shape=jax.ShapeDtypeStruct((m, d), _F32),
        compiler_params=pltpu.CompilerParams(
            dimension_semantics=("parallel", "parallel"),
            vmem_limit_bytes=_vmem_limit(est)),
        name="outproj",
    )(o, w, x)


def kernel(x, norm_g, ffn1_w_gate, ffn1_w_up, ffn1_w_down, ffn2_w_gate, ffn2_w_up,
           ffn2_w_down, even_w_in, even_b_forget, even_w_out, odd_w_qkv, odd_w_out,
           final_norm_g):
    batch, seq, d = x.shape
    depth = norm_g.shape[0]
    n_heads = d // HEAD_DIM
    assert n_heads == N_SB_HEADS + N_FOX_HEADS
    bf = lambda w: w.astype(_BF16)
    xf = x.reshape(batch * seq, d)
    for layer in range(depth):
        j = layer // 2
        xf = _ffn(xf, norm_g[layer, 0], bf(ffn1_w_gate[layer]), bf(ffn1_w_up[layer]),
                  bf(ffn1_w_down[layer]))
        if layer % 2 == 0:
            w_in = even_w_in[j]
            w_gate = jnp.zeros((d, V7X_LANES), _BF16).at[:, :N_FOX_HEADS].set(
                bf(w_in[:, 3 * d:]))
            qkv, gate_pre = _proj(xf, norm_g[layer, 1], bf(w_in[:, :3 * d]), w_gate=w_gate)
            cf_col, cf_row = _gates(gate_pre, even_b_forget[j], batch, seq)
            o = _attn_even(qkv, cf_col, cf_row, batch, seq)
            xf = _outproj(o, bf(even_w_out[j]), xf)
        else:
            qkv = _proj(xf, norm_g[layer, 1], bf(odd_w_qkv[j]), rope_seq=seq, rope_cols=2 * d)
            o = _attn_dilated(qkv, batch, seq, n_heads)
            xf = _outproj(o, bf(odd_w_out[j]), xf)
        last = layer == depth - 1
        xf = _ffn(xf, norm_g[layer, 2], bf(ffn2_w_gate[layer]), bf(ffn2_w_up[layer]),
                  bf(ffn2_w_down[layer]), final_norm_g if last else None)
    return xf.reshape(batch, seq, d)
```

```python
import functools

import jax
import jax.numpy as jnp
from jax import lax
from jax.experimental import pallas as pl
from jax.experimental.pallas import tpu as pltpu

_F32 = jnp.float32
_BF16 = jnp.bfloat16

HEAD_DIM = 128
N_SB_HEADS = 8
N_FOX_HEADS = 8
ROPE_DIMS = HEAD_DIM // 4
ROPE_THETA = 500000.0
RMS_EPS = 1e-6
NEG_INF = -1e30
LOG2_E = 1.4426950408889634
DIL_W1, DIL_W2, DIL_D2, DIL_D3 = 128, 512, 4, 16

V7X_LANES = 128
V7X_VMEM_BYTES = 64 * 1024 * 1024
VMEM_CAP_BYTES = 56 * 1024 * 1024

ATT_BLOCK = 256
ATT_CHAINS = 4


def _vmem_limit(block_bytes):
    return int(min(VMEM_CAP_BYTES, max(32 * 1024 * 1024, block_bytes * 5 // 4)))


def _rms(x, g):
    ms = jnp.mean(x * x, axis=-1, keepdims=True)
    return x * lax.rsqrt(ms + RMS_EPS) * g


def _log1p_exp_neg_abs(z):
    return jnp.log(1.0 + jnp.exp(-jnp.abs(z)))


def _ffn_kernel(*refs, final_norm, n_f):
    if final_norm:
        x_ref, g_ref, wg_ref, wu_ref, wd_ref, gf_ref, o_ref, n_ref = refs
    else:
        x_ref, g_ref, wg_ref, wu_ref, wd_ref, o_ref, n_ref = refs
    j = pl.program_id(1)

    @pl.when(j == 0)
    def _():
        x = x_ref[...]
        n_ref[...] = _rms(x, g_ref[...]).astype(_BF16)
        o_ref[...] = x

    n = n_ref[...]
    gate = jnp.dot(n, wg_ref[...], preferred_element_type=_F32)
    up = jnp.dot(n, wu_ref[...], preferred_element_type=_F32)
    h = (gate * jax.nn.sigmoid(gate)) * (up * 0.5)
    o_ref[...] += jnp.dot(h.astype(_BF16), wd_ref[...], preferred_element_type=_F32)

    if final_norm:
        @pl.when(j == n_f - 1)
        def _():
            o_ref[...] = _rms(o_ref[...], gf_ref[...])


def _ffn(x, g, wg, wu, wd, final_g=None, *, tm=512, tf=512):
    m, d = x.shape
    f = wg.shape[1]
    assert m % tm == 0 and f % tf == 0
    final_norm = final_g is not None
    in_specs = [
        pl.BlockSpec((tm, d), lambda i, j: (i, 0)),
        pl.BlockSpec((1, d), lambda i, j: (0, 0)),
        pl.BlockSpec((d, tf), lambda i, j: (0, j)),
        pl.BlockSpec((d, tf), lambda i, j: (0, j)),
        pl.BlockSpec((tf, d), lambda i, j: (j, 0)),
    ]
    args = [x, g.reshape(1, d), wg, wu, wd]
    if final_norm:
        in_specs.append(pl.BlockSpec((1, d), lambda i, j: (0, 0)))
        args.append(final_g.reshape(1, d))
    est = 2 * (2 * tm * d * 4) + tm * d * 2 + 2 * 3 * d * tf * 2 + 4 * tm * tf * 4
    return pl.pallas_call(
        functools.partial(_ffn_kernel, final_norm=final_norm, n_f=f // tf),
        grid=(m // tm, f // tf),
        in_specs=in_specs,
        out_specs=pl.BlockSpec((tm, d), lambda i, j: (i, 0)),
        out_shape=jax.ShapeDtypeStruct((m, d), _F32),
        scratch_shapes=[pltpu.VMEM((tm, d), _BF16)],
        compiler_params=pltpu.CompilerParams(
            dimension_semantics=("parallel", "arbitrary"),
            vmem_limit_bytes=_vmem_limit(est)),
        name="ffn_final" if final_norm else "ffn",
    )(*args)


def _proj_gate_kernel(x_ref, g_ref, w_ref, wf_ref, o_ref, gate_ref, n_ref):
    j = pl.program_id(1)

    @pl.when(j == 0)
    def _():
        n = _rms(x_ref[...], g_ref[...]).astype(_BF16)
        n_ref[...] = n
        gate_ref[...] = jnp.dot(n, wf_ref[...], preferred_element_type=_F32)

    o_ref[...] = jnp.dot(n_ref[...], w_ref[...],
                         preferred_element_type=_F32).astype(o_ref.dtype)


def _proj_rope_kernel(x_ref, g_ref, w_ref, cos_ref, sin_lo_ref, sin_hi_ref, o_ref, n_ref,
                      *, rope_blocks):
    j = pl.program_id(1)

    @pl.when(j == 0)
    def _():
        n_ref[...] = _rms(x_ref[...], g_ref[...]).astype(_BF16)

    y = jnp.dot(n_ref[...], w_ref[...], preferred_element_type=_F32)
    half = ROPE_DIMS // 2
    tn = y.shape[1]

    @pl.when(j < rope_blocks)
    def _():
        from_hi = pltpu.roll(y, tn - half, 1)
        from_lo = pltpu.roll(y, half, 1)
        r = y * cos_ref[...] + from_hi * sin_lo_ref[...] + from_lo * sin_hi_ref[...]
        o_ref[...] = r.astype(o_ref.dtype)

    @pl.when(j >= rope_blocks)
    def _():
        o_ref[...] = y.astype(o_ref.dtype)


def _rope_tables(seq, tn):
    half = ROPE_DIMS // 2
    freqs = ROPE_THETA ** (-jnp.arange(half, dtype=_F32) / half)
    ang = jnp.arange(seq, dtype=_F32)[:, None] * freqs[None, :]
    cos, sin = jnp.cos(ang), jnp.sin(ang)
    pad = jnp.zeros((seq, HEAD_DIM - ROPE_DIMS), _F32)
    zero = jnp.zeros((seq, half), _F32)
    cos_t = jnp.concatenate([cos, cos, pad + 1.0], axis=1)
    sin_lo = jnp.concatenate([-sin, zero, pad], axis=1)
    sin_hi = jnp.concatenate([zero, sin, pad], axis=1)
    reps = tn // HEAD_DIM
    return tuple(jnp.tile(t, (1, reps)) for t in (cos_t, sin_lo, sin_hi))


def _proj(x, g, w, *, w_gate=None, rope_seq=None, rope_cols=0, tm=1024, tn=512):
    m, d = x.shape
    n_out = w.shape[1]
    assert m % tm == 0 and n_out % tn == 0
    in_specs = [
        pl.BlockSpec((tm, d), lambda i, j: (i, 0)),
        pl.BlockSpec((1, d), lambda i, j: (0, 0)),
        pl.BlockSpec((d, tn), lambda i, j: (0, j)),
    ]
    args = [x, g.reshape(1, d), w]
    out_specs = pl.BlockSpec((tm, tn), lambda i, j: (i, j))
    out_shape = jax.ShapeDtypeStruct((m, n_out), _BF16)
    est = 2 * tm * d * 4 + tm * d * 2 + 2 * d * tn * 2 + 2 * tm * tn * 2 + 2 * tm * tn * 4
    if w_gate is not None:
        ng = w_gate.shape[1]
        in_specs.append(pl.BlockSpec((d, ng), lambda i, j: (0, 0)))
        args.append(w_gate)
        out_specs = (out_specs, pl.BlockSpec((tm, ng), lambda i, j: (i, 0)))
        out_shape = (out_shape, jax.ShapeDtypeStruct((m, ng), _F32))
        body = _proj_gate_kernel
        name = "proj_gate"
    else:
        assert rope_seq % tm == 0 and rope_cols % tn == 0
        seq_blocks = rope_seq // tm
        tab_spec = pl.BlockSpec((tm, tn), lambda i, j: (i % seq_blocks, 0))
        in_specs += [tab_spec] * 3
        args += list(_rope_tables(rope_seq, tn))
        est += 3 * 2 * tm * tn * 4
        body = functools.partial(_proj_rope_kernel, rope_blocks=rope_cols // tn)
        name = "proj_rope"
    return pl.pallas_call(
        body,
        grid=(m // tm, n_out // tn),
        in_specs=in_specs,
        out_specs=out_specs,
        out_shape=out_shape,
        scratch_shapes=[pltpu.VMEM((tm, d), _BF16)],
        compiler_params=pltpu.CompilerParams(
            dimension_semantics=("parallel", "arbitrary"),
            vmem_limit_bytes=_vmem_limit(est)),
        name=name,
    )(*args)


def _split3(x):
    hi = x.astype(_BF16)
    r = x - hi.astype(_F32)
    mid = r.astype(_BF16)
    lo = (r - mid.astype(_F32)).astype(_BF16)
    return hi, mid, lo


def _gates_kernel(gp_ref, b_ref, col_ref, row_ref, *, blk):
    seq = gp_ref.shape[1]
    z = gp_ref[0] + b_ref[...]
    log_f = jnp.minimum(z, 0.0) - _log1p_exp_neg_abs(z)
    r = lax.broadcasted_iota(jnp.int32, (blk, blk), 0)
    c = lax.broadcasted_iota(jnp.int32, (blk, blk), 1)
    tri = jnp.where(c <= r, 1.0, 0.0).astype(_BF16)
    carry = jnp.zeros((1, log_f.shape[1]), _F32)
    for s in range(seq // blk):
        part = log_f[s * blk:(s + 1) * blk]
        cs = carry
        for p in _split3(part):
            cs = cs + jnp.dot(tri, p, preferred_element_type=_F32)
        col_ref[0, s * blk:(s + 1) * blk, :] = cs
        carry = cs[blk - 1:blk, :]
    row_ref[0] = jnp.transpose(col_ref[0])[:row_ref.shape[1], :]


def _gates(gate_pre, b_forget, batch, seq):
    ng = gate_pre.shape[1]
    nh = b_forget.shape[0]
    b_pad = jnp.zeros((1, ng), _F32).at[0, :nh].set(b_forget)
    return pl.pallas_call(
        functools.partial(_gates_kernel, blk=ATT_BLOCK),
        grid=(batch,),
        in_specs=[pl.BlockSpec((1, seq, ng), lambda b: (b, 0, 0)),
                  pl.BlockSpec((1, ng), lambda b: (0, 0))],
        out_specs=(pl.BlockSpec((1, seq, ng), lambda b: (b, 0, 0)),
                   pl.BlockSpec((1, nh, seq), lambda b: (b, 0, 0))),
        out_shape=(jax.ShapeDtypeStruct((batch, seq, ng), _F32),
                   jax.ShapeDtypeStruct((batch, nh, seq), _F32)),
        compiler_params=pltpu.CompilerParams(dimension_semantics=("parallel",)),
        name="gates",
    )(gate_pre.reshape(batch, seq, ng), b_pad)


def _transpose_bf16(x):
    return jnp.transpose(x.astype(_F32)).astype(_BF16)


def _block_iotas(blk):
    key = lax.broadcasted_iota(jnp.int32, (blk, blk), 0)
    qry = lax.broadcasted_iota(jnp.int32, (blk, blk), 1)
    return key, qry


def _rows(ref, j, blk):
    return ref[pl.ds(pl.multiple_of(j * blk, blk), blk), :]


def _cols(ref, j, blk):
    return ref[:, pl.ds(pl.multiple_of(j * blk, blk), blk)]


def _chain_sweep(logits, finish, states, chains, base, n_shared):
    def wave(w):
        return [(c, base + c - w, w, w == 0) for c in range(w, chains)]

    def shared(step):
        j = jnp.maximum(base - 1 - step, 0)
        return [(c, j, c + 1 + step, False) for c in range(chains)]

    states = list(states)
    us = logits(wave(0))
    for w in range(chains):
        ahead = logits(wave(w + 1) if w + 1 < chains else shared(0))
        new = finish(wave(w), us, states[w:])
        states[w:] = new
        us = ahead

    def body(step, carry):
        us, sts = carry
        ahead = logits(shared(step + 1))
        return tuple(ahead), tuple(finish(shared(step), list(us), list(sts)))

    return lax.fori_loop(0, n_shared, body, (tuple(us), tuple(states)))[1]


def _sb_head(qt, k_ref, vt_ref, o_ref, tile, blk, chains):
    scale2 = HEAD_DIM ** -0.5 * LOG2_E
    key, qry = _block_iotas(blk)
    strict = key < qry
    suffix = jnp.where(qry > key, 1.0, 0.0).astype(_BF16)

    def gates(z2, diagonal):
        log_beta = jnp.minimum(z2, 0.0) - jnp.log2(1.0 + jnp.exp2(-jnp.abs(z2)))
        log_not_beta = log_beta - z2
        if diagonal:
            log_not_beta = jnp.where(strict, log_not_beta, 0.0)
        hi = log_not_beta.astype(_BF16)
        lo = (log_not_beta - hi.astype(_F32)).astype(_BF16)
        return log_beta, log_not_beta, hi, lo

    def logits(items):
        return [jnp.dot(_rows(k_ref, j, blk), qt[c], preferred_element_type=_F32) * scale2
                for c, j, _, _ in items]

    def finish(items, z2, states):
        g = [gates(z, diag) for z, (_, _, _, diag) in zip(z2, items)]
        after = [jnp.dot(suffix, hi, preferred_element_type=_F32)
                 + jnp.dot(suffix, lo, preferred_element_type=_F32) + carry
                 for (_, _, hi, lo), (carry, _) in zip(g, states)]
        new = []
        for (log_beta, log_not_beta, _, _), aft, (carry, acc), (_, j, _, diag) in zip(
                g, after, states, items):
            w = jnp.exp2(log_beta + aft)
            if diag:
                w = jnp.where(strict, w, 0.0)
            acc = acc + jnp.dot(_cols(vt_ref, j, blk), w.astype(_BF16),
                                preferred_element_type=_F32)
            new.append((carry + jnp.sum(log_not_beta, axis=0, keepdims=True), acc))
        return new

    init = (jnp.zeros((1, blk), _F32), jnp.zeros((HEAD_DIM, blk), _F32))
    base = tile * chains
    states = _chain_sweep(logits, finish, [init] * chains, chains, base, base)
    for c in range(chains):
        o_ref[c * blk:(c + 1) * blk, :] = jnp.transpose(states[c][1]).astype(o_ref.dtype)


def _softmax_finish(values, shifts=None):
    def finish(items, us, states):
        mids = []
        for u, (c, _, _, _), (m, l, _) in zip(us, items, states):
            u_max = jnp.max(u, axis=0, keepdims=True)
            shift = None if shifts is None else shifts[c]
            if shift is not None:
                u_max = u_max + shift
            m_new = jnp.maximum(m, u_max)
            alpha = jnp.exp2(m - m_new)
            p = jnp.exp2(u - m_new if shift is None else u + (shift - m_new))
            mids.append((m_new, alpha, alpha * l + jnp.sum(p, axis=0, keepdims=True),
                         p.astype(_BF16)))
        return [(m_new, l_new, alpha * acc + jnp.dot(values(j), p, preferred_element_type=_F32))
                for (m_new, alpha, l_new, p), (_, j, _, _), (_, _, acc) in zip(mids, items, states)]
    return finish


def _softmax_init(blk):
    return (jnp.full((1, blk), NEG_INF, _F32), jnp.zeros((1, blk), _F32),
            jnp.zeros((HEAD_DIM, blk), _F32))


def _softmax_store(o_ref, states, blk):
    for c, (_, l, acc) in enumerate(states):
        o_ref[c * blk:(c + 1) * blk, :] = jnp.transpose(acc / l).astype(o_ref.dtype)


FOX_BIAS_PARTS = 3


def _fox_augment_keys(k_ref, cfc_ref, hh, kaug_ref):
    seq = k_ref.shape[0]
    lane = lax.broadcasted_iota(jnp.int32, (seq, cfc_ref.shape[2]), 1)
    cf = jnp.sum(jnp.where(lane == hh, cfc_ref[0], 0.0), axis=1, keepdims=True)
    parts = _split3(cf * -(HEAD_DIM ** 0.5))
    lane = lax.broadcasted_iota(jnp.int32, (seq, HEAD_DIM), 1)
    extra = jnp.zeros((seq, HEAD_DIM), _F32)
    for i, part in enumerate(parts):
        extra = jnp.where(lane == i, part.astype(_F32), extra)
    kaug_ref[:, :HEAD_DIM] = k_ref[...]
    kaug_ref[:, HEAD_DIM:] = extra.astype(_BF16)


def _fox_head(qt, kaug_ref, vt_ref, cfr_ref, o_ref, hh, tile, blk, chains):
    scale2 = HEAD_DIM ** -0.5 * LOG2_E
    key, qry = _block_iotas(blk)
    causal = key <= qry
    sub = lax.broadcasted_iota(jnp.int32, (HEAD_DIM, blk), 0)
    ones_rows = jnp.where(sub < FOX_BIAS_PARTS, 1.0, 0.0).astype(_BF16)
    base = tile * chains
    qt_aug = [jnp.concatenate([qt[c], ones_rows], axis=0) for c in range(chains)]
    cf_q = [cfr_ref[0, hh, pl.ds(base + c, 1), :] * LOG2_E for c in range(chains)]

    def logit(c, j, dist, diagonal):
        u = jnp.dot(_rows(kaug_ref, j, blk), qt_aug[c], preferred_element_type=_F32) * scale2
        return jnp.where(causal, u, NEG_INF) if diagonal else u

    states = _chain_sweep(lambda items: [logit(*it) for it in items],
                          _softmax_finish(lambda j: _cols(vt_ref, j, blk), cf_q),
                          [_softmax_init(blk)] * chains, chains, base, base)
    _softmax_store(o_ref, states, blk)


def _attn_even_kernel(q_ref, k_ref, v_ref, cfc_ref, cfr_ref, o_ref, vt_ref, kaug_ref,
                      *, blk, chains):
    h = pl.program_id(1)
    tile = pl.program_id(2)

    @pl.when(tile == 0)
    def _():
        vt_ref[...] = _transpose_bf16(v_ref[...])

    @pl.when((tile == 0) & (h >= N_SB_HEADS))
    def _():
        _fox_augment_keys(k_ref, cfc_ref, h - N_SB_HEADS, kaug_ref)

    qt_all = _transpose_bf16(q_ref[...])
    qt = [qt_all[:, c * blk:(c + 1) * blk] for c in range(chains)]

    @pl.when(h < N_SB_HEADS)
    def _():
        _sb_head(qt, k_ref, vt_ref, o_ref, tile, blk, chains)

    @pl.when(h >= N_SB_HEADS)
    def _():
        _fox_head(qt, kaug_ref, vt_ref, cfr_ref, o_ref, h - N_SB_HEADS, tile, blk, chains)


def _attn_specs(batch, seq, n_heads, rows):
    nq = seq // rows
    q_spec = pl.BlockSpec((rows, HEAD_DIM), lambda b, h, i: (b * nq + i, h))
    k_spec = pl.BlockSpec((seq, HEAD_DIM), lambda b, h, i: (b, n_heads + h))
    v_spec = pl.BlockSpec((seq, HEAD_DIM), lambda b, h, i: (b, 2 * n_heads + h))
    o_spec = pl.BlockSpec((rows, HEAD_DIM), lambda b, h, i: (b * nq + i, h))
    return nq, q_spec, k_spec, v_spec, o_spec


def _attn_even(qkv, cf_col, cf_row, batch, seq, *, blk=ATT_BLOCK, chains=ATT_CHAINS):
    n_heads = N_SB_HEADS + N_FOX_HEADS
    nq, q_spec, k_spec, v_spec, o_spec = _attn_specs(batch, seq, n_heads, blk * chains)
    ng = cf_col.shape[2]
    nk = seq // blk
    cf_row = cf_row.reshape(batch, N_FOX_HEADS, nk, blk)
    return pl.pallas_call(
        functools.partial(_attn_even_kernel, blk=blk, chains=chains),
        grid=(batch, n_heads, nq),
        in_specs=[q_spec, k_spec, v_spec,
                  pl.BlockSpec((1, seq, ng), lambda b, h, i: (b, 0, 0)),
                  pl.BlockSpec((1, N_FOX_HEADS, nk, blk), lambda b, h, i: (b, 0, 0, 0))],
        out_specs=o_spec,
        out_shape=jax.ShapeDtypeStruct((batch * seq, n_heads * HEAD_DIM), _BF16),
        scratch_shapes=[pltpu.VMEM((HEAD_DIM, seq), _BF16),
                        pltpu.VMEM((seq, 2 * HEAD_DIM), _BF16)],
        compiler_params=pltpu.CompilerParams(
            dimension_semantics=("parallel", "parallel", "arbitrary")),
        name="attn_even",
    )(qkv, qkv, qkv, cf_col, cf_row)


def _dilated_bias_tables(blk):
    n_near = (DIL_W2 + blk - 1) // blk + 1
    assert blk % DIL_D3 == 0 and n_near * blk - (blk - 1) > DIL_W2
    d = jnp.arange(n_near + 1, dtype=jnp.int32)[:, None, None]
    s = jnp.arange(blk, dtype=jnp.int32)[None, :, None]
    t = jnp.arange(blk, dtype=jnp.int32)[None, None, :]
    delta = d * blk + t - s
    count = ((delta <= DIL_W1).astype(_F32)
             + ((delta % DIL_D2 == 0) & (delta <= DIL_W2)).astype(_F32)
             + (delta % DIL_D3 == 0).astype(_F32))
    count = jnp.where(delta >= 0, count, 0.0)
    return jnp.where(count > 0, jnp.log2(jnp.maximum(count, 1.0)), NEG_INF), n_near


def _attn_dilated_kernel(q_ref, k_ref, v_ref, bias_ref, o_ref, vt_ref, *, blk, chains,
                         n_near):
    tile = pl.program_id(2)

    @pl.when(tile == 0)
    def _():
        vt_ref[...] = _transpose_bf16(v_ref[...])

    scale2 = HEAD_DIM ** -0.5 * LOG2_E
    qt_all = _transpose_bf16(q_ref[...])
    qt = [qt_all[:, c * blk:(c + 1) * blk] for c in range(chains)]
    base = tile * chains

    def logit(c, j, dist, diagonal):
        table = min(dist, n_near) if isinstance(dist, int) else jnp.minimum(dist, n_near)
        return (jnp.dot(_rows(k_ref, j, blk), qt[c], preferred_element_type=_F32) * scale2
                + bias_ref[table])

    states = _chain_sweep(lambda items: [logit(*it) for it in items],
                          _softmax_finish(lambda j: _cols(vt_ref, j, blk)),
                          [_softmax_init(blk)] * chains, chains, base, base)
    _softmax_store(o_ref, states, blk)


def _attn_dilated(qkv, batch, seq, n_heads, *, blk=ATT_BLOCK, chains=ATT_CHAINS):
    assert seq <= DIL_D3 * DIL_W1
    nq, q_spec, k_spec, v_spec, o_spec = _attn_specs(batch, seq, n_heads, blk * chains)
    bias, n_near = _dilated_bias_tables(blk)
    return pl.pallas_call(
        functools.partial(_attn_dilated_kernel, blk=blk, chains=chains, n_near=n_near),
        grid=(batch, n_heads, nq),
        in_specs=[q_spec, k_spec, v_spec,
                  pl.BlockSpec(bias.shape, lambda b, h, i: (0, 0, 0))],
        out_specs=o_spec,
        out_shape=jax.ShapeDtypeStruct((batch * seq, n_heads * HEAD_DIM), _BF16),
        scratch_shapes=[pltpu.VMEM((HEAD_DIM, seq), _BF16)],
        compiler_params=pltpu.CompilerParams(
            dimension_semantics=("parallel", "parallel", "arbitrary")),
        name="attn_dilated",
    )(qkv, qkv, qkv, bias)


def _outproj_kernel(o_ref, w_ref, x_ref, y_ref):
    y_ref[...] = x_ref[...] + jnp.dot(o_ref[...], w_ref[...], preferred_element_type=_F32)


def _outproj(o, w, x, *, tm=1024, tn=512):
    m, d_in = o.shape
    d = w.shape[1]
    est = 2 * tm * d_in * 2 + 2 * d_in * tn * 2 + 4 * tm * tn * 4
    return pl.pallas_call(
        _outproj_kernel,
        grid=(m // tm, d // tn),
        in_specs=[pl.BlockSpec((tm, d_in), lambda i, j: (i, 0)),
                  pl.BlockSpec((d_in, tn), lambda i, j: (0, j)),
                  pl.BlockSpec((tm, tn), lambda i, j: (i, j))],
        out_specs=pl.BlockSpec((tm, tn), lambda i, j: (i, j)),
        out_shape=jax.ShapeDtypeStruct((m, d), _F32),
        compiler_params=pltpu.CompilerParams(
            dimension_semantics=("parallel", "parallel"),
            vmem_limit_bytes=_vmem_limit(est)),
        name="outproj",
    )(o, w, x)


def kernel(x, norm_g, ffn1_w_gate, ffn1_w_up, ffn1_w_down, ffn2_w_gate, ffn2_w_up,
           ffn2_w_down, even_w_in, even_b_forget, even_w_out, odd_w_qkv, odd_w_out,
           final_norm_g):
    batch, seq, d = x.shape
    depth = norm_g.shape[0]
    n_heads = d // HEAD_DIM
    assert n_heads == N_SB_HEADS + N_FOX_HEADS
    bf = lambda w: w.astype(_BF16)
    xf = x.reshape(batch * seq, d)
    for layer in range(depth):
        j = layer // 2
        xf = _ffn(xf, norm_g[layer, 0], bf(ffn1_w_gate[layer]), bf(ffn1_w_up[layer]),
                  bf(ffn1_w_down[layer]))
        if layer % 2 == 0:
            w_in = even_w_in[j]
            w_gate = jnp.zeros((d, V7X_LANES), _BF16).at[:, :N_FOX_HEADS].set(
                bf(w_in[:, 3 * d:]))
            qkv, gate_pre = _proj(xf, norm_g[layer, 1], bf(w_in[:, :3 * d]), w_gate=w_gate)
            cf_col, cf_row = _gates(gate_pre, even_b_forget[j], batch, seq)
            o = _attn_even(qkv, cf_col, cf_row, batch, seq)
            xf = _outproj(o, bf(even_w_out[j]), xf)
        else:
            qkv = _proj(xf, norm_g[layer, 1], bf(odd_w_qkv[j]), rope_seq=seq, rope_cols=2 * d)
            o = _attn_dilated(qkv, batch, seq, n_heads)
            xf = _outproj(o, bf(odd_w_out[j]), xf)
        last = layer == depth - 1
        xf = _ffn(xf, norm_g[layer, 2], bf(ffn2_w_gate[layer]), bf(ffn2_w_up[layer]),
                  bf(ffn2_w_down[layer]), final_norm_g if last else None)
    return xf.reshape(batch, seq, d)
```

```python
import functools

import jax
import jax.numpy as jnp
from jax import lax
from jax.experimental import pallas as pl
from jax.experimental.pallas import tpu as pltpu

_F32 = jnp.float32
_BF16 = jnp.bfloat16

HEAD_DIM = 128
N_SB_HEADS = 8
N_FOX_HEADS = 8
ROPE_DIMS = HEAD_DIM // 4
ROPE_THETA = 500000.0
RMS_EPS = 1e-6
NEG_INF = -1e30
LOG2_E = 1.4426950408889634
DIL_W1, DIL_W2, DIL_D2, DIL_D3 = 128, 512, 4, 16

V7X_LANES = 128
V7X_VMEM_BYTES = 64 * 1024 * 1024
VMEM_CAP_BYTES = 56 * 1024 * 1024

PROJ_CHUNK = 256
ATT_BLOCK = 256
ATT_CHAINS = 8


def _vmem_limit(block_bytes):
    return int(min(VMEM_CAP_BYTES, max(32 * 1024 * 1024, block_bytes * 5 // 4)))


def _rms(x, g):
    ms = jnp.mean(x * x, axis=-1, keepdims=True)
    return x * lax.rsqrt(ms + RMS_EPS) * g


def _log1p_exp_neg_abs(z):
    return jnp.log(1.0 + jnp.exp(-jnp.abs(z)))


def _ffn_kernel(*refs, final_norm, n_f):
    if final_norm:
        x_ref, g_ref, wg_ref, wu_ref, wd_ref, gf_ref, o_ref, n_ref = refs
    else:
        x_ref, g_ref, wg_ref, wu_ref, wd_ref, o_ref, n_ref = refs
    j = pl.program_id(1)

    @pl.when(j == 0)
    def _():
        x = x_ref[...]
        n_ref[...] = _rms(x, g_ref[...]).astype(_BF16)
        o_ref[...] = x

    n = n_ref[...]
    gate = jnp.dot(n, wg_ref[...].astype(_BF16), preferred_element_type=_F32)
    up = jnp.dot(n, wu_ref[...].astype(_BF16), preferred_element_type=_F32)
    h = (gate * jax.nn.sigmoid(gate)) * (up * 0.5)
    o_ref[...] += jnp.dot(h.astype(_BF16), wd_ref[...].astype(_BF16),
                          preferred_element_type=_F32)

    if final_norm:
        @pl.when(j == n_f - 1)
        def _():
            o_ref[...] = _rms(o_ref[...], gf_ref[...])


def _ffn(x, g, wg, wu, wd, layer, final_g=None, *, tm=1024, tf=256):
    m, d = x.shape
    f = wg.shape[2]
    assert m % tm == 0 and f % tf == 0
    final_norm = final_g is not None
    in_specs = [
        pl.BlockSpec((tm, d), lambda i, j: (i, 0)),
        pl.BlockSpec((1, d), lambda i, j: (0, 0)),
        pl.BlockSpec((None, d, tf), lambda i, j: (layer, 0, j)),
        pl.BlockSpec((None, d, tf), lambda i, j: (layer, 0, j)),
        pl.BlockSpec((None, tf, d), lambda i, j: (layer, j, 0)),
    ]
    args = [x, g.reshape(1, d), wg, wu, wd]
    if final_norm:
        in_specs.append(pl.BlockSpec((1, d), lambda i, j: (0, 0)))
        args.append(final_g.reshape(1, d))
    est = (2 * (2 * tm * d * 4) + tm * d * 2 + 2 * 3 * d * tf * 4 + 3 * d * tf * 2
           + 4 * tm * tf * 4)
    return pl.pallas_call(
        functools.partial(_ffn_kernel, final_norm=final_norm, n_f=f // tf),
        grid=(m // tm, f // tf),
        in_specs=in_specs,
        out_specs=pl.BlockSpec((tm, d), lambda i, j: (i, 0)),
        out_shape=jax.ShapeDtypeStruct((m, d), _F32),
        scratch_shapes=[pltpu.VMEM((tm, d), _BF16)],
        compiler_params=pltpu.CompilerParams(
            dimension_semantics=("parallel", "arbitrary"),
            vmem_limit_bytes=_vmem_limit(est)),
        name="ffn_final" if final_norm else "ffn",
    )(*args)


def _proj_gate_kernel(x_ref, g_ref, w_ref, wf_ref, o_ref, gate_ref, n_ref):
    j = pl.program_id(1)

    @pl.when(j == 0)
    def _():
        n = _rms(x_ref[...], g_ref[...]).astype(_BF16)
        n_ref[...] = n
        gate_ref[...] = jnp.dot(n, wf_ref[...], preferred_element_type=_F32)

    o_ref[...] = jnp.dot(n_ref[...], w_ref[...].astype(_BF16),
                         preferred_element_type=_F32).astype(o_ref.dtype)


def _proj_rope_kernel(x_ref, g_ref, w_ref, cos_ref, sin_lo_ref, sin_hi_ref, o_ref, n_ref,
                      *, rope_blocks):
    j = pl.program_id(1)

    @pl.when(j == 0)
    def _():
        n_ref[...] = _rms(x_ref[...], g_ref[...]).astype(_BF16)

    half = ROPE_DIMS // 2
    tn = o_ref.shape[1]
    chunks = [slice(c, c + PROJ_CHUNK) for c in range(0, tn, PROJ_CHUNK)]

    def project(cols):
        return jnp.dot(n_ref[...], w_ref[:, cols].astype(_BF16), preferred_element_type=_F32)

    @pl.when(j < rope_blocks)
    def _():
        reps = PROJ_CHUNK // HEAD_DIM
        cos, sin_lo, sin_hi = (jnp.tile(t[...], (1, reps))
                               for t in (cos_ref, sin_lo_ref, sin_hi_ref))
        for cols in chunks:
            y = project(cols)
            r = (y * cos + pltpu.roll(y, PROJ_CHUNK - half, 1) * sin_lo
                 + pltpu.roll(y, half, 1) * sin_hi)
            o_ref[:, cols] = r.astype(o_ref.dtype)

    @pl.when(j >= rope_blocks)
    def _():
        for cols in chunks:
            o_ref[:, cols] = project(cols).astype(o_ref.dtype)


def _rope_tables(seq):
    half = ROPE_DIMS // 2
    freqs = ROPE_THETA ** (-jnp.arange(half, dtype=_F32) / half)
    ang = jnp.arange(seq, dtype=_F32)[:, None] * freqs[None, :]
    cos, sin = jnp.cos(ang), jnp.sin(ang)
    pad = jnp.zeros((seq, HEAD_DIM - ROPE_DIMS), _F32)
    zero = jnp.zeros((seq, half), _F32)
    cos_t = jnp.concatenate([cos, cos, pad + 1.0], axis=1)
    sin_lo = jnp.concatenate([-sin, zero, pad], axis=1)
    sin_hi = jnp.concatenate([zero, sin, pad], axis=1)
    return cos_t, sin_lo, sin_hi


def _proj(x, g, w, layer, n_out, *, w_gate=None, rope_seq=None, rope_cols=0, tm=1024,
          tn=1024):
    m, d = x.shape
    assert m % tm == 0 and n_out % tn == 0 and n_out <= w.shape[2] and tn % PROJ_CHUNK == 0
    in_specs = [
        pl.BlockSpec((tm, d), lambda i, j: (i, 0)),
        pl.BlockSpec((1, d), lambda i, j: (0, 0)),
        pl.BlockSpec((None, d, tn), lambda i, j: (layer, 0, j)),
    ]
    args = [x, g.reshape(1, d), w]
    out_specs = pl.BlockSpec((tm, tn), lambda i, j: (i, j))
    out_shape = jax.ShapeDtypeStruct((m, n_out), _BF16)
    est = (2 * tm * d * 4 + tm * d * 2 + 2 * d * tn * 4 + d * tn * 2 + 2 * tm * tn * 2
           + 2 * tm * tn * 4)
    if w_gate is not None:
        ng = w_gate.shape[1]
        in_specs.append(pl.BlockSpec((d, ng), lambda i, j: (0, 0)))
        args.append(w_gate)
        out_specs = (out_specs, pl.BlockSpec((tm, ng), lambda i, j: (i, 0)))
        out_shape = (out_shape, jax.ShapeDtypeStruct((m, ng), _F32))
        body = _proj_gate_kernel
        name = "proj_gate"
    else:
        assert rope_seq % tm == 0 and rope_cols % tn == 0
        seq_blocks = rope_seq // tm
        tab_spec = pl.BlockSpec((tm, HEAD_DIM), lambda i, j: (i % seq_blocks, 0))
        in_specs += [tab_spec] * 3
        args += list(_rope_tables(rope_seq))
        est += 3 * 2 * tm * HEAD_DIM * 4
        body = functools.partial(_proj_rope_kernel, rope_blocks=rope_cols // tn)
        name = "proj_rope"
    return pl.pallas_call(
        body,
        grid=(m // tm, n_out // tn),
        in_specs=in_specs,
        out_specs=out_specs,
        out_shape=out_shape,
        scratch_shapes=[pltpu.VMEM((tm, d), _BF16)],
        compiler_params=pltpu.CompilerParams(
            dimension_semantics=("parallel", "arbitrary"),
            vmem_limit_bytes=_vmem_limit(est)),
        name=name,
    )(*args)


def _split3(x):
    hi = x.astype(_BF16)
    r = x - hi.astype(_F32)
    mid = r.astype(_BF16)
    lo = (r - mid.astype(_F32)).astype(_BF16)
    return hi, mid, lo


def _gates_kernel(gp_ref, b_ref, col_ref, row_ref, *, blk):
    seq = gp_ref.shape[1]
    z = gp_ref[0] + b_ref[...]
    log_f = jnp.minimum(z, 0.0) - _log1p_exp_neg_abs(z)
    r = lax.broadcasted_iota(jnp.int32, (blk, blk), 0)
    c = lax.broadcasted_iota(jnp.int32, (blk, blk), 1)
    tri = jnp.where(c <= r, 1.0, 0.0).astype(_BF16)
    carry = jnp.zeros((1, log_f.shape[1]), _F32)
    for s in range(seq // blk):
        part = log_f[s * blk:(s + 1) * blk]
        cs = carry
        for p in _split3(part):
            cs = cs + jnp.dot(tri, p, preferred_element_type=_F32)
        col_ref[0, s * blk:(s + 1) * blk, :] = cs
        carry = cs[blk - 1:blk, :]
    row_ref[0] = jnp.transpose(col_ref[0])[:row_ref.shape[1], :]


def _gates(gate_pre, b_forget, batch, seq):
    ng = gate_pre.shape[1]
    nh = b_forget.shape[0]
    b_pad = jnp.zeros((1, ng), _F32).at[0, :nh].set(b_forget)
    return pl.pallas_call(
        functools.partial(_gates_kernel, blk=ATT_BLOCK),
        grid=(batch,),
        in_specs=[pl.BlockSpec((1, seq, ng), lambda b: (b, 0, 0)),
                  pl.BlockSpec((1, ng), lambda b: (0, 0))],
        out_specs=(pl.BlockSpec((1, seq, ng), lambda b: (b, 0, 0)),
                   pl.BlockSpec((1, nh, seq), lambda b: (b, 0, 0))),
        out_shape=(jax.ShapeDtypeStruct((batch, seq, ng), _F32),
                   jax.ShapeDtypeStruct((batch, nh, seq), _F32)),
        compiler_params=pltpu.CompilerParams(dimension_semantics=("parallel",)),
        name="gates",
    )(gate_pre.reshape(batch, seq, ng), b_pad)


def _transpose_bf16(x):
    return jnp.transpose(x.astype(_F32)).astype(_BF16)


def _block_iotas(blk):
    key = lax.broadcasted_iota(jnp.int32, (blk, blk), 0)
    qry = lax.broadcasted_iota(jnp.int32, (blk, blk), 1)
    return key, qry


def _rows(ref, j, blk):
    return ref[pl.ds(pl.multiple_of(j * blk, blk), blk), :]


def _cols(ref, j, blk):
    return ref[:, pl.ds(pl.multiple_of(j * blk, blk), blk)]


def _chain_sweep(logits, finish, states, chains, base, n_shared):
    def wave(w):
        return [(c, base + c - w, w, w == 0) for c in range(w, chains)]

    def shared(step):
        j = jnp.maximum(base - 1 - step, 0)
        return [(c, j, c + 1 + step, False) for c in range(chains)]

    states = list(states)
    us = logits(wave(0))
    for w in range(chains):
        if w + 1 < chains:
            ahead = logits(wave(w + 1))
        elif n_shared is not None:
            ahead = logits(shared(0))
        new = finish(wave(w), us, states[w:])
        states[w:] = new
        us = ahead
    if n_shared is None:
        return states

    def body(step, carry):
        us, sts = carry
        ahead = logits(shared(step + 1))
        return tuple(ahead), tuple(finish(shared(step), list(us), list(sts)))

    return lax.fori_loop(0, n_shared, body, (tuple(us), tuple(states)))[1]


def _sb_head(qt, k_ref, vt_ref, o_ref, tile, blk, chains, single_tile):
    scale2 = HEAD_DIM ** -0.5 * LOG2_E
    key, qry = _block_iotas(blk)
    strict = key < qry
    suffix = jnp.where(qry > key, 1.0, 0.0).astype(_BF16)

    def gates(z2, diagonal):
        log_beta = jnp.minimum(z2, 0.0) - jnp.log2(1.0 + jnp.exp2(-jnp.abs(z2)))
        log_not_beta = log_beta - z2
        if diagonal:
            log_not_beta = jnp.where(strict, log_not_beta, 0.0)
        hi = log_not_beta.astype(_BF16)
        lo = (log_not_beta - hi.astype(_F32)).astype(_BF16)
        return log_beta, log_not_beta, hi, lo

    def logits(items):
        return [jnp.dot(_rows(k_ref, j, blk), qt[c], preferred_element_type=_F32) * scale2
                for c, j, _, _ in items]

    def finish(items, z2, states):
        g = [gates(z, diag) for z, (_, _, _, diag) in zip(z2, items)]
        after = [jnp.dot(suffix, hi, preferred_element_type=_F32)
                 + jnp.dot(suffix, lo, preferred_element_type=_F32) + carry
                 for (_, _, hi, lo), (carry, _) in zip(g, states)]
        new = []
        for (log_beta, log_not_beta, _, _), aft, (carry, acc), (_, j, _, diag) in zip(
                g, after, states, items):
            w = jnp.exp2(log_beta + aft)
            if diag:
                w = jnp.where(strict, w, 0.0)
            acc = acc + jnp.dot(_cols(vt_ref, j, blk), w.astype(_BF16),
                                preferred_element_type=_F32)
            new.append((carry + jnp.sum(log_not_beta, axis=0, keepdims=True), acc))
        return new

    init = (jnp.zeros((1, blk), _F32), jnp.zeros((HEAD_DIM, blk), _F32))
    base = tile * chains
    states = _chain_sweep(logits, finish, [init] * chains, chains, base,
                          None if single_tile else base)
    for c in range(chains):
        o_ref[c * blk:(c + 1) * blk, :] = jnp.transpose(states[c][1]).astype(o_ref.dtype)


def _softmax_finish(values, shifts=None):
    def finish(items, us, states):
        mids = []
        for u, (c, _, _, _), (m, l, _) in zip(us, items, states):
            u_max = jnp.max(u, axis=0, keepdims=True)
            shift = None if shifts is None else shifts[c]
            if shift is not None:
                u_max = u_max + shift
            m_new = jnp.maximum(m, u_max)
            alpha = jnp.exp2(m - m_new)
            p = jnp.exp2(u - m_new if shift is None else u + (shift - m_new))
            mids.append((m_new, alpha, alpha * l + jnp.sum(p, axis=0, keepdims=True),
                         p.astype(_BF16)))
        return [(m_new, l_new, alpha * acc + jnp.dot(values(j), p, preferred_element_type=_F32))
                for (m_new, alpha, l_new, p), (_, j, _, _), (_, _, acc) in zip(mids, items, states)]
    return finish


def _softmax_init(blk):
    return (jnp.full((1, blk), NEG_INF, _F32), jnp.zeros((1, blk), _F32),
            jnp.zeros((HEAD_DIM, blk), _F32))


def _softmax_store(o_ref, states, blk):
    for c, (_, l, acc) in enumerate(states):
        o_ref[c * blk:(c + 1) * blk, :] = jnp.transpose(acc / l).astype(o_ref.dtype)


FOX_BIAS_PARTS = 3


def _fox_augment_keys(k_ref, cfc_ref, hh, kaug_ref):
    seq = k_ref.shape[0]
    lane = lax.broadcasted_iota(jnp.int32, (seq, cfc_ref.shape[2]), 1)
    cf = jnp.sum(jnp.where(lane == hh, cfc_ref[0], 0.0), axis=1, keepdims=True)
    parts = _split3(cf * -(HEAD_DIM ** 0.5))
    lane = lax.broadcasted_iota(jnp.int32, (seq, HEAD_DIM), 1)
    extra = jnp.zeros((seq, HEAD_DIM), _F32)
    for i, part in enumerate(parts):
        extra = jnp.where(lane == i, part.astype(_F32), extra)
    kaug_ref[:, :HEAD_DIM] = k_ref[...]
    kaug_ref[:, HEAD_DIM:] = extra.astype(_BF16)


def _fox_head(qt, kaug_ref, vt_ref, cfr_ref, o_ref, hh, tile, blk, chains, single_tile):
    scale2 = HEAD_DIM ** -0.5 * LOG2_E
    key, qry = _block_iotas(blk)
    causal = key <= qry
    sub = lax.broadcasted_iota(jnp.int32, (HEAD_DIM, blk), 0)
    ones_rows = jnp.where(sub < FOX_BIAS_PARTS, 1.0, 0.0).astype(_BF16)
    base = tile * chains
    qt_aug = [jnp.concatenate([qt[c], ones_rows], axis=0) for c in range(chains)]
    cf_q = [cfr_ref[0, hh, pl.ds(base + c, 1), :] * LOG2_E for c in range(chains)]

    def logit(c, j, dist, diagonal):
        u = jnp.dot(_rows(kaug_ref, j, blk), qt_aug[c], preferred_element_type=_F32) * scale2
        return jnp.where(causal, u, NEG_INF) if diagonal else u

    states = _chain_sweep(lambda items: [logit(*it) for it in items],
                          _softmax_finish(lambda j: _cols(vt_ref, j, blk), cf_q),
                          [_softmax_init(blk)] * chains, chains, base,
                          None if single_tile else base)
    _softmax_store(o_ref, states, blk)


def _attn_even_kernel(q_ref, k_ref, v_ref, cfc_ref, cfr_ref, o_ref, vt_ref, kaug_ref,
                      *, blk, chains, single_tile):
    h = pl.program_id(1)
    tile = pl.program_id(2)

    @pl.when(tile == 0)
    def _():
        vt_ref[...] = _transpose_bf16(v_ref[...])

    @pl.when((tile == 0) & (h >= N_SB_HEADS))
    def _():
        _fox_augment_keys(k_ref, cfc_ref, h - N_SB_HEADS, kaug_ref)

    qt_all = _transpose_bf16(q_ref[...])
    qt = [qt_all[:, c * blk:(c + 1) * blk] for c in range(chains)]

    @pl.when(h < N_SB_HEADS)
    def _():
        _sb_head(qt, k_ref, vt_ref, o_ref, tile, blk, chains, single_tile)

    @pl.when(h >= N_SB_HEADS)
    def _():
        _fox_head(qt, kaug_ref, vt_ref, cfr_ref, o_ref, h - N_SB_HEADS, tile, blk, chains,
                  single_tile)


def _attn_specs(batch, seq, n_heads, rows):
    nq = seq // rows
    q_spec = pl.BlockSpec((rows, HEAD_DIM), lambda b, h, i: (b * nq + i, h))
    k_spec = pl.BlockSpec((seq, HEAD_DIM), lambda b, h, i: (b, n_heads + h))
    v_spec = pl.BlockSpec((seq, HEAD_DIM), lambda b, h, i: (b, 2 * n_heads + h))
    o_spec = pl.BlockSpec((rows, HEAD_DIM), lambda b, h, i: (b * nq + i, h))
    return nq, q_spec, k_spec, v_spec, o_spec


def _attn_even(qkv, cf_col, cf_row, batch, seq, *, blk=ATT_BLOCK, chains=ATT_CHAINS):
    n_heads = N_SB_HEADS + N_FOX_HEADS
    nq, q_spec, k_spec, v_spec, o_spec = _attn_specs(batch, seq, n_heads, blk * chains)
    ng = cf_col.shape[2]
    nk = seq // blk
    cf_row = cf_row.reshape(batch, N_FOX_HEADS, nk, blk)
    return pl.pallas_call(
        functools.partial(_attn_even_kernel, blk=blk, chains=chains, single_tile=nq == 1),
        grid=(batch, n_heads, nq),
        in_specs=[q_spec, k_spec, v_spec,
                  pl.BlockSpec((1, seq, ng), lambda b, h, i: (b, 0, 0)),
                  pl.BlockSpec((1, N_FOX_HEADS, nk, blk), lambda b, h, i: (b, 0, 0, 0))],
        out_specs=o_spec,
        out_shape=jax.ShapeDtypeStruct((batch * seq, n_heads * HEAD_DIM), _BF16),
        scratch_shapes=[pltpu.VMEM((HEAD_DIM, seq), _BF16),
                        pltpu.VMEM((seq, 2 * HEAD_DIM), _BF16)],
        compiler_params=pltpu.CompilerParams(
            dimension_semantics=("parallel", "parallel", "arbitrary")),
        name="attn_even",
    )(qkv, qkv, qkv, cf_col, cf_row)


def _dilated_bias_tables(blk):
    n_near = (DIL_W2 + blk - 1) // blk + 1
    assert blk % DIL_D3 == 0 and n_near * blk - (blk - 1) > DIL_W2
    d = jnp.arange(n_near + 1, dtype=jnp.int32)[:, None, None]
    s = jnp.arange(blk, dtype=jnp.int32)[None, :, None]
    t = jnp.arange(blk, dtype=jnp.int32)[None, None, :]
    delta = d * blk + t - s
    count = ((delta <= DIL_W1).astype(_F32)
             + ((delta % DIL_D2 == 0) & (delta <= DIL_W2)).astype(_F32)
             + (delta % DIL_D3 == 0).astype(_F32))
    count = jnp.where(delta >= 0, count, 0.0)
    return jnp.where(count > 0, jnp.log2(jnp.maximum(count, 1.0)), NEG_INF), n_near


def _attn_dilated_kernel(q_ref, k_ref, v_ref, bias_ref, o_ref, vt_ref, *, blk, chains,
                         n_near, single_tile):
    tile = pl.program_id(2)

    @pl.when(tile == 0)
    def _():
        vt_ref[...] = _transpose_bf16(v_ref[...])

    scale2 = HEAD_DIM ** -0.5 * LOG2_E
    qt_all = _transpose_bf16(q_ref[...])
    qt = [qt_all[:, c * blk:(c + 1) * blk] for c in range(chains)]
    base = tile * chains

    def logit(c, j, dist, diagonal):
        table = min(dist, n_near) if isinstance(dist, int) else jnp.minimum(dist, n_near)
        return (jnp.dot(_rows(k_ref, j, blk), qt[c], preferred_element_type=_F32) * scale2
                + bias_ref[table])

    states = _chain_sweep(lambda items: [logit(*it) for it in items],
                          _softmax_finish(lambda j: _cols(vt_ref, j, blk)),
                          [_softmax_init(blk)] * chains, chains, base,
                          None if single_tile else base)
    _softmax_store(o_ref, states, blk)


def _attn_dilated(qkv, batch, seq, n_heads, *, blk=ATT_BLOCK, chains=ATT_CHAINS):
    assert seq <= DIL_D3 * DIL_W1
    nq, q_spec, k_spec, v_spec, o_spec = _attn_specs(batch, seq, n_heads, blk * chains)
    bias, n_near = _dilated_bias_tables(blk)
    return pl.pallas_call(
        functools.partial(_attn_dilated_kernel, blk=blk, chains=chains, n_near=n_near,
                          single_tile=nq == 1),
        grid=(batch, n_heads, nq),
        in_specs=[q_spec, k_spec, v_spec,
                  pl.BlockSpec(bias.shape, lambda b, h, i: (0, 0, 0))],
        out_specs=o_spec,
        out_shape=jax.ShapeDtypeStruct((batch * seq, n_heads * HEAD_DIM), _BF16),
        scratch_shapes=[pltpu.VMEM((HEAD_DIM, seq), _BF16)],
        compiler_params=pltpu.CompilerParams(
            dimension_semantics=("parallel", "parallel", "arbitrary")),
        name="attn_dilated",
    )(qkv, qkv, qkv, bias)


def _outproj_kernel(o_ref, w_ref, x_ref, y_ref):
    y_ref[...] = x_ref[...] + jnp.dot(o_ref[...], w_ref[...].astype(_BF16),
                                      preferred_element_type=_F32)


def _outproj(o, w, layer, x, *, tm=1024, tn=512):
    m, d_in = o.shape
    d = w.shape[2]
    est = 2 * tm * d_in * 2 + 2 * d_in * tn * 4 + d_in * tn * 2 + 4 * tm * tn * 4
    return pl.pallas_call(
        _outproj_kernel,
        grid=(m // tm, d // tn),
        in_specs=[pl.BlockSpec((tm, d_in), lambda i, j: (i, 0)),
                  pl.BlockSpec((None, d_in, tn), lambda i, j: (layer, 0, j)),
                  pl.BlockSpec((tm, tn), lambda i, j: (i, j))],
        out_specs=pl.BlockSpec((tm, tn), lambda i, j: (i, j)),
        out_shape=jax.ShapeDtypeStruct((m, d), _F32),
        compiler_params=pltpu.CompilerParams(
            dimension_semantics=("parallel", "parallel"),
            vmem_limit_bytes=_vmem_limit(est)),
        name="outproj",
    )(o, w, x)


def kernel(x, norm_g, ffn1_w_gate, ffn1_w_up, ffn1_w_down, ffn2_w_gate, ffn2_w_up,
           ffn2_w_down, even_w_in, even_b_forget, even_w_out, odd_w_qkv, odd_w_out,
           final_norm_g):
    batch, seq, d = x.shape
    depth = norm_g.shape[0]
    n_heads = d // HEAD_DIM
    assert n_heads == N_SB_HEADS + N_FOX_HEADS
    xf = x.reshape(batch * seq, d)
    for layer in range(depth):
        j = layer // 2
        xf = _ffn(xf, norm_g[layer, 0], ffn1_w_gate, ffn1_w_up, ffn1_w_down, layer)
        if layer % 2 == 0:
            w_gate = jnp.zeros((d, V7X_LANES), _BF16).at[:, :N_FOX_HEADS].set(
                even_w_in[j, :, 3 * d:].astype(_BF16))
            qkv, gate_pre = _proj(xf, norm_g[layer, 1], even_w_in, j, 3 * d, w_gate=w_gate)
            cf_col, cf_row = _gates(gate_pre, even_b_forget[j], batch, seq)
            o = _attn_even(qkv, cf_col, cf_row, batch, seq)
            xf = _outproj(o, even_w_out, j, xf)
        else:
            qkv = _proj(xf, norm_g[layer, 1], odd_w_qkv, j, 3 * d, rope_seq=seq,
                        rope_cols=2 * d)
            o = _attn_dilated(qkv, batch, seq, n_heads)
            xf = _outproj(o, odd_w_out, j, xf)
        last = layer == depth - 1
        xf = _ffn(xf, norm_g[layer, 2], ffn2_w_gate, ffn2_w_up, ffn2_w_down, layer,
                  final_norm_g if last else None)
    return xf.reshape(batch, seq, d)
```

```python
import functools

import jax
import jax.numpy as jnp
from jax import lax
from jax.experimental import pallas as pl
from jax.experimental.pallas import tpu as pltpu

_F32 = jnp.float32
_BF16 = jnp.bfloat16

HEAD_DIM = 128
N_SB_HEADS = 8
N_FOX_HEADS = 8
ROPE_DIMS = HEAD_DIM // 4
ROPE_THETA = 500000.0
RMS_EPS = 1e-6
NEG_INF = -1e30
LOG2_E = 1.4426950408889634
DIL_W1, DIL_W2, DIL_D2, DIL_D3 = 128, 512, 4, 16

V7X_LANES = 128
V7X_VMEM_BYTES = 64 * 1024 * 1024
V7X_SCOPED_VMEM_DEFAULT = 32 * 1024 * 1024
VMEM_CAP_BYTES = V7X_VMEM_BYTES * 7 // 8

PROJ_CHUNK = 256
ATT_BLOCK = 256
ATT_CHAINS = 8


def _vmem_limit(block_bytes):
    return int(min(VMEM_CAP_BYTES, max(V7X_SCOPED_VMEM_DEFAULT, block_bytes * 5 // 4)))


def _rms(x, g):
    ms = jnp.mean(x * x, axis=-1, keepdims=True)
    return x * lax.rsqrt(ms + RMS_EPS) * g


def _log1p_exp_neg_abs(z):
    return jnp.log(1.0 + jnp.exp(-jnp.abs(z)))


def _ffn_kernel(*refs, final_norm, layer, n_f, tf):
    if final_norm:
        (x_ref, g_ref, wg_hbm, wu_hbm, wd_hbm, gf_ref, o_ref,
         n_ref, wg_buf, wu_buf, wd_buf, sem) = refs
    else:
        x_ref, g_ref, wg_hbm, wu_hbm, wd_hbm, o_ref, n_ref, wg_buf, wu_buf, wd_buf, sem = refs

    def weight_copies(j, slot):
        cols = pl.ds(pl.multiple_of(j * tf, tf), tf)
        return (pltpu.make_async_copy(wg_hbm.at[layer, :, cols], wg_buf.at[slot], sem.at[0, slot]),
                pltpu.make_async_copy(wu_hbm.at[layer, :, cols], wu_buf.at[slot], sem.at[1, slot]),
                pltpu.make_async_copy(wd_hbm.at[layer, cols, :], wd_buf.at[slot], sem.at[2, slot]))

    for copy in weight_copies(0, 0):
        copy.start()
    x = x_ref[...]
    n_ref[...] = _rms(x, g_ref[...]).astype(_BF16)
    o_ref[...] = x

    def step(j, carry):
        slot = j % 2

        @pl.when(j + 1 < n_f)
        def _():
            for copy in weight_copies(j + 1, 1 - slot):
                copy.start()

        for copy in weight_copies(j, slot):
            copy.wait()
        n = n_ref[...]
        gate = jnp.dot(n, wg_buf[slot].astype(_BF16), preferred_element_type=_F32)
        up = jnp.dot(n, wu_buf[slot].astype(_BF16), preferred_element_type=_F32)
        h = (gate * jax.nn.sigmoid(gate)) * (up * 0.5)
        o_ref[...] += jnp.dot(h.astype(_BF16), wd_buf[slot].astype(_BF16),
                              preferred_element_type=_F32)
        return carry

    lax.fori_loop(0, n_f, step, 0)
    if final_norm:
        o_ref[...] = _rms(o_ref[...], gf_ref[...])


def _ffn(x, g, wg, wu, wd, layer, final_g=None, *, tm=1024, tf=256):
    m, d = x.shape
    f = wg.shape[2]
    assert m % tm == 0 and f % tf == 0
    final_norm = final_g is not None
    hbm = pl.BlockSpec(memory_space=pl.ANY)
    in_specs = [pl.BlockSpec((tm, d), lambda i: (i, 0)), pl.BlockSpec((1, d), lambda i: (0, 0)),
                hbm, hbm, hbm]
    args = [x, g.reshape(1, d), wg, wu, wd]
    if final_norm:
        in_specs.append(pl.BlockSpec((1, d), lambda i: (0, 0)))
        args.append(final_g.reshape(1, d))
    est = (2 * (2 * tm * d * 4) + tm * d * 2 + 2 * 3 * d * tf * 4 + 3 * d * tf * 2
           + 4 * tm * tf * 4)
    return pl.pallas_call(
        functools.partial(_ffn_kernel, final_norm=final_norm, layer=layer, n_f=f // tf, tf=tf),
        grid=(m // tm,),
        in_specs=in_specs,
        out_specs=pl.BlockSpec((tm, d), lambda i: (i, 0)),
        out_shape=jax.ShapeDtypeStruct((m, d), _F32),
        scratch_shapes=[pltpu.VMEM((tm, d), _BF16),
                        pltpu.VMEM((2, d, tf), _F32), pltpu.VMEM((2, d, tf), _F32),
                        pltpu.VMEM((2, tf, d), _F32), pltpu.SemaphoreType.DMA((3, 2))],
        compiler_params=pltpu.CompilerParams(
            dimension_semantics=("parallel",),
            vmem_limit_bytes=_vmem_limit(est)),
        name="ffn_final" if final_norm else "ffn",
    )(*args)


def _store_heads(o_ref, col, y):
    for c in range(0, y.shape[1], HEAD_DIM):
        o_ref[(col + c) // HEAD_DIM] = y[:, c:c + HEAD_DIM].astype(o_ref.dtype)


def _proj_gate_kernel(x_ref, g_ref, w_ref, wf_ref, o_ref, gate_ref, n_ref):
    j = pl.program_id(1)

    @pl.when(j == 0)
    def _():
        n = _rms(x_ref[...], g_ref[...]).astype(_BF16)
        n_ref[...] = n
        gate_ref[...] = jnp.dot(n, wf_ref[...], preferred_element_type=_F32)

    for c in range(0, w_ref.shape[1], PROJ_CHUNK):
        y = jnp.dot(n_ref[...], w_ref[:, c:c + PROJ_CHUNK].astype(_BF16),
                    preferred_element_type=_F32)
        _store_heads(o_ref, c, y)


def _proj_rope_kernel(x_ref, g_ref, w_ref, cos_ref, sin_lo_ref, sin_hi_ref, o_ref, n_ref,
                      *, rope_blocks):
    j = pl.program_id(1)

    @pl.when(j == 0)
    def _():
        n_ref[...] = _rms(x_ref[...], g_ref[...]).astype(_BF16)

    half = ROPE_DIMS // 2
    chunks = range(0, w_ref.shape[1], PROJ_CHUNK)

    def project(col):
        return jnp.dot(n_ref[...], w_ref[:, col:col + PROJ_CHUNK].astype(_BF16),
                       preferred_element_type=_F32)

    @pl.when(j < rope_blocks)
    def _():
        reps = PROJ_CHUNK // HEAD_DIM
        cos, sin_lo, sin_hi = (jnp.tile(t[...], (1, reps))
                               for t in (cos_ref, sin_lo_ref, sin_hi_ref))
        for col in chunks:
            y = project(col)
            _store_heads(o_ref, col, y * cos + pltpu.roll(y, PROJ_CHUNK - half, 1) * sin_lo
                         + pltpu.roll(y, half, 1) * sin_hi)

    @pl.when(j >= rope_blocks)
    def _():
        for col in chunks:
            _store_heads(o_ref, col, project(col))


def _rope_tables(seq):
    half = ROPE_DIMS // 2
    freqs = ROPE_THETA ** (-jnp.arange(half, dtype=_F32) / half)
    ang = jnp.arange(seq, dtype=_F32)[:, None] * freqs[None, :]
    cos, sin = jnp.cos(ang), jnp.sin(ang)
    pad = jnp.zeros((seq, HEAD_DIM - ROPE_DIMS), _F32)
    zero = jnp.zeros((seq, half), _F32)
    cos_t = jnp.concatenate([cos, cos, pad + 1.0], axis=1)
    sin_lo = jnp.concatenate([-sin, zero, pad], axis=1)
    sin_hi = jnp.concatenate([zero, sin, pad], axis=1)
    return cos_t, sin_lo, sin_hi


def _proj(x, g, w, layer, n_out, *, w_gate=None, rope_seq=None, rope_cols=0, tm=1024,
          tn=1024):
    m, d = x.shape
    assert m % tm == 0 and n_out % tn == 0 and n_out <= w.shape[2] and tn % PROJ_CHUNK == 0
    in_specs = [
        pl.BlockSpec((tm, d), lambda i, j: (i, 0)),
        pl.BlockSpec((1, d), lambda i, j: (0, 0)),
        pl.BlockSpec((None, d, tn), lambda i, j: (layer, 0, j)),
    ]
    args = [x, g.reshape(1, d), w]
    out_specs = pl.BlockSpec((tn // HEAD_DIM, tm, HEAD_DIM), lambda i, j: (j, i, 0))
    out_shape = jax.ShapeDtypeStruct((n_out // HEAD_DIM, m, HEAD_DIM), _BF16)
    est = (2 * tm * d * 4 + tm * d * 2 + 2 * d * tn * 4 + d * tn * 2 + 2 * tm * tn * 2
           + 2 * tm * tn * 4)
    if w_gate is not None:
        ng = w_gate.shape[1]
        in_specs.append(pl.BlockSpec((d, ng), lambda i, j: (0, 0)))
        args.append(w_gate)
        out_specs = (out_specs, pl.BlockSpec((tm, ng), lambda i, j: (i, 0)))
        out_shape = (out_shape, jax.ShapeDtypeStruct((m, ng), _F32))
        body = _proj_gate_kernel
        name = "proj_gate"
    else:
        assert rope_seq % tm == 0 and rope_cols % tn == 0
        seq_blocks = rope_seq // tm
        tab_spec = pl.BlockSpec((tm, HEAD_DIM), lambda i, j: (i % seq_blocks, 0))
        in_specs += [tab_spec] * 3
        args += list(_rope_tables(rope_seq))
        est += 3 * 2 * tm * HEAD_DIM * 4
        body = functools.partial(_proj_rope_kernel, rope_blocks=rope_cols // tn)
        name = "proj_rope"
    return pl.pallas_call(
        body,
        grid=(m // tm, n_out // tn),
        in_specs=in_specs,
        out_specs=out_specs,
        out_shape=out_shape,
        scratch_shapes=[pltpu.VMEM((tm, d), _BF16)],
        compiler_params=pltpu.CompilerParams(
            dimension_semantics=("parallel", "arbitrary"),
            vmem_limit_bytes=_vmem_limit(est)),
        name=name,
    )(*args)


def _split3(x):
    hi = x.astype(_BF16)
    r = x - hi.astype(_F32)
    mid = r.astype(_BF16)
    lo = (r - mid.astype(_F32)).astype(_BF16)
    return hi, mid, lo


def _gates_kernel(gp_ref, b_ref, col_ref, row_ref, *, blk):
    seq = gp_ref.shape[1]
    z = gp_ref[0] + b_ref[...]
    log_f = jnp.minimum(z, 0.0) - _log1p_exp_neg_abs(z)
    r = lax.broadcasted_iota(jnp.int32, (blk, blk), 0)
    c = lax.broadcasted_iota(jnp.int32, (blk, blk), 1)
    tri = jnp.where(c <= r, 1.0, 0.0).astype(_BF16)
    carry = jnp.zeros((1, log_f.shape[1]), _F32)
    for s in range(seq // blk):
        part = log_f[s * blk:(s + 1) * blk]
        cs = carry
        for p in _split3(part):
            cs = cs + jnp.dot(tri, p, preferred_element_type=_F32)
        col_ref[0, s * blk:(s + 1) * blk, :] = cs
        carry = cs[blk - 1:blk, :]
    row_ref[0] = jnp.transpose(col_ref[0])[:row_ref.shape[1], :]


def _gates(gate_pre, b_forget, batch, seq):
    ng = gate_pre.shape[1]
    nh = b_forget.shape[0]
    b_pad = jnp.zeros((1, ng), _F32).at[0, :nh].set(b_forget)
    return pl.pallas_call(
        functools.partial(_gates_kernel, blk=ATT_BLOCK),
        grid=(batch,),
        in_specs=[pl.BlockSpec((1, seq, ng), lambda b: (b, 0, 0)),
                  pl.BlockSpec((1, ng), lambda b: (0, 0))],
        out_specs=(pl.BlockSpec((1, seq, ng), lambda b: (b, 0, 0)),
                   pl.BlockSpec((1, nh, seq), lambda b: (b, 0, 0))),
        out_shape=(jax.ShapeDtypeStruct((batch, seq, ng), _F32),
                   jax.ShapeDtypeStruct((batch, nh, seq), _F32)),
        compiler_params=pltpu.CompilerParams(dimension_semantics=("parallel",)),
        name="gates",
    )(gate_pre.reshape(batch, seq, ng), b_pad)


def _transpose_bf16(x):
    return jnp.transpose(x)


def _neg_abs(x):
    bits = lax.bitcast_convert_type(x, jnp.uint32) | jnp.uint32(0x80000000)
    return lax.bitcast_convert_type(bits, _F32)


def _block_iotas(blk):
    key = lax.broadcasted_iota(jnp.int32, (blk, blk), 0)
    qry = lax.broadcasted_iota(jnp.int32, (blk, blk), 1)
    return key, qry


def _rows(ref, j, blk):
    return ref[pl.ds(pl.multiple_of(j * blk, blk), blk), :]


def _cols(ref, j, blk):
    return ref[:, pl.ds(pl.multiple_of(j * blk, blk), blk)]


def _chain_sweep(logits, finish, states, chains, base, n_shared):
    def wave(w):
        return [(c, base + c - w, w, w == 0) for c in range(w, chains)]

    def shared(step):
        j = jnp.maximum(base - 1 - step, 0)
        return [(c, j, c + 1 + step, False) for c in range(chains)]

    states = list(states)
    us = logits(wave(0))
    for w in range(chains):
        if w + 1 < chains:
            ahead = logits(wave(w + 1))
        elif n_shared is not None:
            ahead = logits(shared(0))
        new = finish(wave(w), us, states[w:])
        states[w:] = new
        us = ahead
    if n_shared is None:
        return states

    def body(step, carry):
        us, sts = carry
        ahead = logits(shared(step + 1))
        return tuple(ahead), tuple(finish(shared(step), list(us), list(sts)))

    return lax.fori_loop(0, n_shared, body, (tuple(us), tuple(states)))[1]


def _sb_head(qt, k_ref, vt_ref, o_ref, tile, blk, chains, single_tile):
    scale2 = HEAD_DIM ** -0.5 * LOG2_E
    key, qry = _block_iotas(blk)
    strict = key < qry
    suffix = jnp.where(qry > key, 1.0, 0.0).astype(_BF16)

    def gates(z2, diagonal):
        log_beta = jnp.minimum(z2, 0.0) - jnp.log2(1.0 + jnp.exp2(_neg_abs(z2)))
        log_not_beta = log_beta - z2
        if diagonal:
            log_not_beta = jnp.where(strict, log_not_beta, 0.0)
        hi = log_not_beta.astype(_BF16)
        lo = (log_not_beta - hi.astype(_F32)).astype(_BF16)
        return log_beta, log_not_beta, hi, lo

    def logits(items):
        return [jnp.dot(_rows(k_ref, j, blk), qt[c], preferred_element_type=_F32) * scale2
                for c, j, _, _ in items]

    def finish(items, z2, states):
        g = [gates(z, diag) for z, (_, _, _, diag) in zip(z2, items)]
        after = [jnp.dot(suffix, hi, preferred_element_type=_F32)
                 + jnp.dot(suffix, lo, preferred_element_type=_F32) + carry
                 for (_, _, hi, lo), (carry, _) in zip(g, states)]
        new = []
        for (log_beta, log_not_beta, _, _), aft, (carry, acc), (_, j, _, diag) in zip(
                g, after, states, items):
            w = jnp.exp2(log_beta + aft)
            if diag:
                w = jnp.where(strict, w, 0.0)
            acc = acc + jnp.dot(_cols(vt_ref, j, blk), w.astype(_BF16),
                                preferred_element_type=_F32)
            new.append((carry + jnp.sum(log_not_beta, axis=0, keepdims=True), acc))
        return new

    init = (jnp.zeros((1, blk), _F32), jnp.zeros((HEAD_DIM, blk), _F32))
    base = tile * chains
    states = _chain_sweep(logits, finish, [init] * chains, chains, base,
                          None if single_tile else base)
    for c in range(chains):
        o_ref[c * blk:(c + 1) * blk, :] = jnp.transpose(states[c][1]).astype(o_ref.dtype)


def _softmax_finish(values, shifts=None):
    def finish(items, us, states):
        mids = []
        for u, (c, _, _, _), (m, l, _) in zip(us, items, states):
            u_max = jnp.max(u, axis=0, keepdims=True)
            shift = None if shifts is None else shifts[c]
            if shift is not None:
                u_max = u_max + shift
            m_new = jnp.maximum(m, u_max)
            alpha = jnp.exp2(m - m_new)
            p = jnp.exp2(u - m_new if shift is None else u + (shift - m_new))
            mids.append((m_new, alpha, alpha * l + jnp.sum(p, axis=0, keepdims=True),
                         p.astype(_BF16)))
        return [(m_new, l_new, alpha * acc + jnp.dot(values(j), p, preferred_element_type=_F32))
                for (m_new, alpha, l_new, p), (_, j, _, _), (_, _, acc) in zip(mids, items, states)]
    return finish


def _softmax_init(blk):
    return (jnp.full((1, blk), NEG_INF, _F32), jnp.zeros((1, blk), _F32),
            jnp.zeros((HEAD_DIM, blk), _F32))


def _softmax_store(o_ref, states, blk):
    for c, (_, l, acc) in enumerate(states):
        o_ref[c * blk:(c + 1) * blk, :] = jnp.transpose(acc / l).astype(o_ref.dtype)


FOX_BIAS_PARTS = 3


def _fox_augment_keys(k_ref, cfc_ref, hh, kaug_ref):
    seq = k_ref.shape[0]
    lane = lax.broadcasted_iota(jnp.int32, (seq, cfc_ref.shape[2]), 1)
    cf = jnp.sum(jnp.where(lane == hh, cfc_ref[0], 0.0), axis=1, keepdims=True)
    parts = _split3(cf * -(HEAD_DIM ** 0.5))
    lane = lax.broadcasted_iota(jnp.int32, (seq, HEAD_DIM), 1)
    extra = jnp.zeros((seq, HEAD_DIM), _F32)
    for i, part in enumerate(parts):
        extra = jnp.where(lane == i, part.astype(_F32), extra)
    kaug_ref[:, :HEAD_DIM] = k_ref[...]
    kaug_ref[:, HEAD_DIM:] = extra.astype(_BF16)


def _fox_head(qt, kaug_ref, vt_ref, cfr_ref, o_ref, hh, tile, blk, chains, single_tile):
    scale2 = HEAD_DIM ** -0.5 * LOG2_E
    key, qry = _block_iotas(blk)
    causal = key <= qry
    sub = lax.broadcasted_iota(jnp.int32, (HEAD_DIM, blk), 0)
    ones_rows = jnp.where(sub < FOX_BIAS_PARTS, 1.0, 0.0).astype(_BF16)
    base = tile * chains
    qt_aug = [jnp.concatenate([qt[c], ones_rows], axis=0) for c in range(chains)]
    cf_q = [cfr_ref[0, hh, pl.ds(base + c, 1), :] * LOG2_E for c in range(chains)]

    def logit(c, j, dist, diagonal):
        u = jnp.dot(_rows(kaug_ref, j, blk), qt_aug[c], preferred_element_type=_F32) * scale2
        return jnp.where(causal, u, NEG_INF) if diagonal else u

    states = _chain_sweep(lambda items: [logit(*it) for it in items],
                          _softmax_finish(lambda j: _cols(vt_ref, j, blk), cf_q),
                          [_softmax_init(blk)] * chains, chains, base,
                          None if single_tile else base)
    _softmax_store(o_ref, states, blk)


def _attn_even_kernel(q_ref, k_ref, v_ref, cfc_ref, cfr_ref, o_ref, vt_ref, kaug_ref,
                      *, blk, chains, single_tile):
    h = pl.program_id(1)
    tile = pl.program_id(2)

    @pl.when(tile == 0)
    def _():
        vt_ref[...] = _transpose_bf16(v_ref[...])

    @pl.when((tile == 0) & (h >= N_SB_HEADS))
    def _():
        _fox_augment_keys(k_ref, cfc_ref, h - N_SB_HEADS, kaug_ref)

    qt_all = _transpose_bf16(q_ref[...])
    qt = [qt_all[:, c * blk:(c + 1) * blk] for c in range(chains)]

    @pl.when(h < N_SB_HEADS)
    def _():
        _sb_head(qt, k_ref, vt_ref, o_ref, tile, blk, chains, single_tile)

    @pl.when(h >= N_SB_HEADS)
    def _():
        _fox_head(qt, kaug_ref, vt_ref, cfr_ref, o_ref, h - N_SB_HEADS, tile, blk, chains,
                  single_tile)


def _attn_specs(batch, seq, n_heads, rows):
    nq = seq // rows
    q_spec = pl.BlockSpec((None, rows, HEAD_DIM), lambda b, h, i: (h, b * nq + i, 0))
    k_spec = pl.BlockSpec((None, seq, HEAD_DIM), lambda b, h, i: (n_heads + h, b, 0))
    v_spec = pl.BlockSpec((None, seq, HEAD_DIM), lambda b, h, i: (2 * n_heads + h, b, 0))
    o_spec = pl.BlockSpec((rows, HEAD_DIM), lambda b, h, i: (b * nq + i, h))
    return nq, q_spec, k_spec, v_spec, o_spec


def _attn_even(qkv, cf_col, cf_row, batch, seq, *, blk=ATT_BLOCK, chains=ATT_CHAINS):
    n_heads = N_SB_HEADS + N_FOX_HEADS
    nq, q_spec, k_spec, v_spec, o_spec = _attn_specs(batch, seq, n_heads, blk * chains)
    ng = cf_col.shape[2]
    nk = seq // blk
    cf_row = cf_row.reshape(batch, N_FOX_HEADS, nk, blk)
    return pl.pallas_call(
        functools.partial(_attn_even_kernel, blk=blk, chains=chains, single_tile=nq == 1),
        grid=(batch, n_heads, nq),
        in_specs=[q_spec, k_spec, v_spec,
                  pl.BlockSpec((1, seq, ng), lambda b, h, i: (b, 0, 0)),
                  pl.BlockSpec((1, N_FOX_HEADS, nk, blk), lambda b, h, i: (b, 0, 0, 0))],
        out_specs=o_spec,
        out_shape=jax.ShapeDtypeStruct((batch * seq, n_heads * HEAD_DIM), _BF16),
        scratch_shapes=[pltpu.VMEM((HEAD_DIM, seq), _BF16),
                        pltpu.VMEM((seq, 2 * HEAD_DIM), _BF16)],
        compiler_params=pltpu.CompilerParams(
            dimension_semantics=("parallel", "parallel", "arbitrary")),
        name="attn_even",
    )(qkv, qkv, qkv, cf_col, cf_row)


def _dilated_bias_tables(blk):
    n_near = (DIL_W2 + blk - 1) // blk + 1
    assert blk % DIL_D3 == 0 and n_near * blk - (blk - 1) > DIL_W2
    d = jnp.arange(n_near + 1, dtype=jnp.int32)[:, None, None]
    s = jnp.arange(blk, dtype=jnp.int32)[None, :, None]
    t = jnp.arange(blk, dtype=jnp.int32)[None, None, :]
    delta = d * blk + t - s
    count = ((delta <= DIL_W1).astype(_F32)
             + ((delta % DIL_D2 == 0) & (delta <= DIL_W2)).astype(_F32)
             + (delta % DIL_D3 == 0).astype(_F32))
    count = jnp.where(delta >= 0, count, 0.0)
    return jnp.where(count > 0, jnp.log2(jnp.maximum(count, 1.0)), NEG_INF), n_near


def _attn_dilated_kernel(q_ref, k_ref, v_ref, bias_ref, o_ref, vt_ref, *, blk, chains,
                         n_near, single_tile):
    tile = pl.program_id(2)

    @pl.when(tile == 0)
    def _():
        vt_ref[...] = _transpose_bf16(v_ref[...])

    scale2 = HEAD_DIM ** -0.5 * LOG2_E
    qt_all = _transpose_bf16(q_ref[...])
    qt = [qt_all[:, c * blk:(c + 1) * blk] for c in range(chains)]
    base = tile * chains

    def logit(c, j, dist, diagonal):
        table = min(dist, n_near) if isinstance(dist, int) else jnp.minimum(dist, n_near)
        return (jnp.dot(_rows(k_ref, j, blk), qt[c], preferred_element_type=_F32) * scale2
                + bias_ref[table])

    states = _chain_sweep(lambda items: [logit(*it) for it in items],
                          _softmax_finish(lambda j: _cols(vt_ref, j, blk)),
                          [_softmax_init(blk)] * chains, chains, base,
                          None if single_tile else base)
    _softmax_store(o_ref, states, blk)


def _attn_dilated(qkv, batch, seq, n_heads, *, blk=ATT_BLOCK, chains=ATT_CHAINS):
    assert seq <= DIL_D3 * DIL_W1
    nq, q_spec, k_spec, v_spec, o_spec = _attn_specs(batch, seq, n_heads, blk * chains)
    bias, n_near = _dilated_bias_tables(blk)
    return pl.pallas_call(
        functools.partial(_attn_dilated_kernel, blk=blk, chains=chains, n_near=n_near,
                          single_tile=nq == 1),
        grid=(batch, n_heads, nq),
        in_specs=[q_spec, k_spec, v_spec,
                  pl.BlockSpec(bias.shape, lambda b, h, i: (0, 0, 0))],
        out_specs=o_spec,
        out_shape=jax.ShapeDtypeStruct((batch * seq, n_heads * HEAD_DIM), _BF16),
        scratch_shapes=[pltpu.VMEM((HEAD_DIM, seq), _BF16)],
        compiler_params=pltpu.CompilerParams(
            dimension_semantics=("parallel", "parallel", "arbitrary")),
        name="attn_dilated",
    )(qkv, qkv, qkv, bias)


def _outproj_kernel(o_ref, w_ref, x_ref, y_ref):
    y_ref[...] = x_ref[...] + jnp.dot(o_ref[...], w_ref[...].astype(_BF16),
                                      preferred_element_type=_F32)


def _outproj(o, w, layer, x, *, tm=1024, tn=512):
    m, d_in = o.shape
    d = w.shape[2]
    est = 2 * tm * d_in * 2 + 2 * d_in * tn * 4 + d_in * tn * 2 + 4 * tm * tn * 4
    return pl.pallas_call(
        _outproj_kernel,
        grid=(m // tm, d // tn),
        in_specs=[pl.BlockSpec((tm, d_in), lambda i, j: (i, 0)),
                  pl.BlockSpec((None, d_in, tn), lambda i, j: (layer, 0, j)),
                  pl.BlockSpec((tm, tn), lambda i, j: (i, j))],
        out_specs=pl.BlockSpec((tm, tn), lambda i, j: (i, j)),
        out_shape=jax.ShapeDtypeStruct((m, d), _F32),
        compiler_params=pltpu.CompilerParams(
            dimension_semantics=("parallel", "parallel"),
            vmem_limit_bytes=_vmem_limit(est)),
        name="outproj",
    )(o, w, x)


def kernel(x, norm_g, ffn1_w_gate, ffn1_w_up, ffn1_w_down, ffn2_w_gate, ffn2_w_up,
           ffn2_w_down, even_w_in, even_b_forget, even_w_out, odd_w_qkv, odd_w_out,
           final_norm_g):
    batch, seq, d = x.shape
    depth = norm_g.shape[0]
    n_heads = d // HEAD_DIM
    assert n_heads == N_SB_HEADS + N_FOX_HEADS
    xf = x.reshape(batch * seq, d)
    for layer in range(depth):
        j = layer // 2
        xf = _ffn(xf, norm_g[layer, 0], ffn1_w_gate, ffn1_w_up, ffn1_w_down, layer)
        if layer % 2 == 0:
            w_gate = jnp.zeros((d, V7X_LANES), _BF16).at[:, :N_FOX_HEADS].set(
                even_w_in[j, :, 3 * d:].astype(_BF16))
            qkv, gate_pre = _proj(xf, norm_g[layer, 1], even_w_in, j, 3 * d, w_gate=w_gate)
            cf_col, cf_row = _gates(gate_pre, even_b_forget[j], batch, seq)
            o = _attn_even(qkv, cf_col, cf_row, batch, seq)
            xf = _outproj(o, even_w_out, j, xf)
        else:
            qkv = _proj(xf, norm_g[layer, 1], odd_w_qkv, j, 3 * d, rope_seq=seq,
                        rope_cols=2 * d)
            o = _attn_dilated(qkv, batch, seq, n_heads)
            xf = _outproj(o, odd_w_out, j, xf)
        last = layer == depth - 1
        xf = _ffn(xf, norm_g[layer, 2], ffn2_w_gate, ffn2_w_up, ffn2_w_down, layer,
                  final_norm_g if last else None)
    return xf.reshape(batch, seq, d)
```

```python
import functools

import jax
import jax.numpy as jnp
from jax import lax
from jax.experimental import pallas as pl
from jax.experimental.pallas import tpu as pltpu

_F32 = jnp.float32
_BF16 = jnp.bfloat16

HEAD_DIM = 128
N_SB_HEADS = 8
N_FOX_HEADS = 8
ROPE_DIMS = HEAD_DIM // 4
ROPE_THETA = 500000.0
RMS_EPS = 1e-6
NEG_INF = -1e30
LOG2_E = 1.4426950408889634
DIL_W1, DIL_W2, DIL_D2, DIL_D3 = 128, 512, 4, 16

V7X_LANES = 128
V7X_VMEM_BYTES = 64 * 1024 * 1024
V7X_SCOPED_VMEM_DEFAULT = 32 * 1024 * 1024
VMEM_CAP_BYTES = V7X_VMEM_BYTES * 15 // 16

FFN_RING = 3
FFN_AHEAD = FFN_RING - 1
PROJ_CHUNK = 256
ATT_BLOCK = 256
ATT_CHAINS = 8


def _vmem_limit(block_bytes):
    return int(min(VMEM_CAP_BYTES, max(V7X_SCOPED_VMEM_DEFAULT, block_bytes * 5 // 4)))


def _rms(x, g):
    ms = jnp.mean(x * x, axis=-1, keepdims=True)
    return x * lax.rsqrt(ms + RMS_EPS) * g


def _log1p_exp_neg_abs(z):
    return jnp.log(1.0 + jnp.exp(-jnp.abs(z)))


def _ffn_kernel(*refs, final_norm, layer, n_f, tf):
    if final_norm:
        (x_ref, g_ref, wg_hbm, wu_hbm, wd_hbm, gf_ref, o_ref,
         n_ref, wg_buf, wu_buf, wd_buf, sem) = refs
    else:
        x_ref, g_ref, wg_hbm, wu_hbm, wd_hbm, o_ref, n_ref, wg_buf, wu_buf, wd_buf, sem = refs

    first = pl.program_id(0) * n_f
    total = pl.num_programs(0) * n_f

    def weight_copies(g):
        slot = g % FFN_RING
        cols = pl.ds(pl.multiple_of((g % n_f) * tf, tf), tf)
        return (pltpu.make_async_copy(wg_hbm.at[layer, :, cols], wg_buf.at[slot], sem.at[0, slot]),
                pltpu.make_async_copy(wu_hbm.at[layer, :, cols], wu_buf.at[slot], sem.at[1, slot]),
                pltpu.make_async_copy(wd_hbm.at[layer, cols, :], wd_buf.at[slot], sem.at[2, slot]))

    @pl.when(first == 0)
    def _():
        for g in range(FFN_AHEAD):
            for copy in weight_copies(g):
                copy.start()

    x = x_ref[...]
    n_ref[...] = _rms(x, g_ref[...]).astype(_BF16)
    o_ref[...] = x

    def step(j, carry):
        g = first + j
        slot = g % FFN_RING

        @pl.when(g + FFN_AHEAD < total)
        def _():
            for copy in weight_copies(g + FFN_AHEAD):
                copy.start()

        for copy in weight_copies(g):
            copy.wait()
        n = n_ref[...]
        gate = jnp.dot(n, wg_buf[slot].astype(_BF16), preferred_element_type=_F32)
        up = jnp.dot(n, wu_buf[slot].astype(_BF16), preferred_element_type=_F32)
        h = (gate * jax.nn.sigmoid(gate)) * (up * 0.5)
        o_ref[...] += jnp.dot(h.astype(_BF16), wd_buf[slot].astype(_BF16),
                              preferred_element_type=_F32)
        return carry

    lax.fori_loop(0, n_f, step, 0)
    if final_norm:
        o_ref[...] = _rms(o_ref[...], gf_ref[...])


def _ffn(x, g, wg, wu, wd, layer, final_g=None, *, tm=1024, tf=256):
    m, d = x.shape
    f = wg.shape[2]
    assert m % tm == 0 and f % tf == 0
    final_norm = final_g is not None
    hbm = pl.BlockSpec(memory_space=pl.ANY)
    in_specs = [pl.BlockSpec((tm, d), lambda i: (i, 0)), pl.BlockSpec((1, d), lambda i: (0, 0)),
                hbm, hbm, hbm]
    args = [x, g.reshape(1, d), wg, wu, wd]
    if final_norm:
        in_specs.append(pl.BlockSpec((1, d), lambda i: (0, 0)))
        args.append(final_g.reshape(1, d))
    est = (2 * (2 * tm * d * 4) + tm * d * 2 + FFN_RING * 3 * d * tf * 4 + 3 * d * tf * 2
           + 4 * tm * tf * 4)
    return pl.pallas_call(
        functools.partial(_ffn_kernel, final_norm=final_norm, layer=layer, n_f=f // tf, tf=tf),
        grid=(m // tm,),
        in_specs=in_specs,
        out_specs=pl.BlockSpec((tm, d), lambda i: (i, 0)),
        out_shape=jax.ShapeDtypeStruct((m, d), _F32),
        scratch_shapes=[pltpu.VMEM((tm, d), _BF16),
                        pltpu.VMEM((FFN_RING, d, tf), _F32), pltpu.VMEM((FFN_RING, d, tf), _F32),
                        pltpu.VMEM((FFN_RING, tf, d), _F32),
                        pltpu.SemaphoreType.DMA((3, FFN_RING))],
        compiler_params=pltpu.CompilerParams(
            dimension_semantics=("arbitrary",),
            vmem_limit_bytes=_vmem_limit(est)),
        name="ffn_final" if final_norm else "ffn",
    )(*args)


def _store_heads(o_ref, col, y):
    for c in range(0, y.shape[1], HEAD_DIM):
        o_ref[(col + c) // HEAD_DIM] = y[:, c:c + HEAD_DIM].astype(o_ref.dtype)


def _proj_gate_kernel(x_ref, g_ref, w_ref, wf_ref, o_ref, gate_ref, n_ref):
    j = pl.program_id(1)

    @pl.when(j == 0)
    def _():
        n = _rms(x_ref[...], g_ref[...]).astype(_BF16)
        n_ref[...] = n
        gate_ref[...] = jnp.dot(n, wf_ref[...], preferred_element_type=_F32)

    for c in range(0, w_ref.shape[1], PROJ_CHUNK):
        y = jnp.dot(n_ref[...], w_ref[:, c:c + PROJ_CHUNK].astype(_BF16),
                    preferred_element_type=_F32)
        _store_heads(o_ref, c, y)


def _proj_rope_kernel(x_ref, g_ref, w_ref, cos_ref, sin_lo_ref, sin_hi_ref, o_ref, n_ref,
                      *, rope_blocks):
    j = pl.program_id(1)

    @pl.when(j == 0)
    def _():
        n_ref[...] = _rms(x_ref[...], g_ref[...]).astype(_BF16)

    half = ROPE_DIMS // 2
    chunks = range(0, w_ref.shape[1], PROJ_CHUNK)

    def project(col):
        return jnp.dot(n_ref[...], w_ref[:, col:col + PROJ_CHUNK].astype(_BF16),
                       preferred_element_type=_F32)

    @pl.when(j < rope_blocks)
    def _():
        reps = PROJ_CHUNK // HEAD_DIM
        cos, sin_lo, sin_hi = (jnp.tile(t[...], (1, reps))
                               for t in (cos_ref, sin_lo_ref, sin_hi_ref))
        for col in chunks:
            y = project(col)
            _store_heads(o_ref, col, y * cos + pltpu.roll(y, PROJ_CHUNK - half, 1) * sin_lo
                         + pltpu.roll(y, half, 1) * sin_hi)

    @pl.when(j >= rope_blocks)
    def _():
        for col in chunks:
            _store_heads(o_ref, col, project(col))


def _rope_tables(seq):
    half = ROPE_DIMS // 2
    freqs = ROPE_THETA ** (-jnp.arange(half, dtype=_F32) / half)
    ang = jnp.arange(seq, dtype=_F32)[:, None] * freqs[None, :]
    cos, sin = jnp.cos(ang), jnp.sin(ang)
    pad = jnp.zeros((seq, HEAD_DIM - ROPE_DIMS), _F32)
    zero = jnp.zeros((seq, half), _F32)
    cos_t = jnp.concatenate([cos, cos, pad + 1.0], axis=1)
    sin_lo = jnp.concatenate([-sin, zero, pad], axis=1)
    sin_hi = jnp.concatenate([zero, sin, pad], axis=1)
    return cos_t, sin_lo, sin_hi


def _proj(x, g, w, layer, n_out, *, w_gate=None, rope_seq=None, rope_cols=0, tm=1024,
          tn=1024):
    m, d = x.shape
    assert m % tm == 0 and n_out % tn == 0 and n_out <= w.shape[2] and tn % PROJ_CHUNK == 0
    in_specs = [
        pl.BlockSpec((tm, d), lambda i, j: (i, 0)),
        pl.BlockSpec((1, d), lambda i, j: (0, 0)),
        pl.BlockSpec((None, d, tn), lambda i, j: (layer, 0, j)),
    ]
    args = [x, g.reshape(1, d), w]
    out_specs = pl.BlockSpec((tn // HEAD_DIM, tm, HEAD_DIM), lambda i, j: (j, i, 0))
    out_shape = jax.ShapeDtypeStruct((n_out // HEAD_DIM, m, HEAD_DIM), _BF16)
    est = (2 * tm * d * 4 + tm * d * 2 + 2 * d * tn * 4 + d * tn * 2 + 2 * tm * tn * 2
           + 2 * tm * tn * 4)
    if w_gate is not None:
        ng = w_gate.shape[1]
        in_specs.append(pl.BlockSpec((d, ng), lambda i, j: (0, 0)))
        args.append(w_gate)
        out_specs = (out_specs, pl.BlockSpec((tm, ng), lambda i, j: (i, 0)))
        out_shape = (out_shape, jax.ShapeDtypeStruct((m, ng), _F32))
        body = _proj_gate_kernel
        name = "proj_gate"
    else:
        assert rope_seq % tm == 0 and rope_cols % tn == 0
        seq_blocks = rope_seq // tm
        tab_spec = pl.BlockSpec((tm, HEAD_DIM), lambda i, j: (i % seq_blocks, 0))
        in_specs += [tab_spec] * 3
        args += list(_rope_tables(rope_seq))
        est += 3 * 2 * tm * HEAD_DIM * 4
        body = functools.partial(_proj_rope_kernel, rope_blocks=rope_cols // tn)
        name = "proj_rope"
    return pl.pallas_call(
        body,
        grid=(m // tm, n_out // tn),
        in_specs=in_specs,
        out_specs=out_specs,
        out_shape=out_shape,
        scratch_shapes=[pltpu.VMEM((tm, d), _BF16)],
        compiler_params=pltpu.CompilerParams(
            dimension_semantics=("parallel", "arbitrary"),
            vmem_limit_bytes=_vmem_limit(est)),
        name=name,
    )(*args)


def _split3(x):
    hi = x.astype(_BF16)
    r = x - hi.astype(_F32)
    mid = r.astype(_BF16)
    lo = (r - mid.astype(_F32)).astype(_BF16)
    return hi, mid, lo


def _gates_kernel(gp_ref, b_ref, col_ref, row_ref, *, blk):
    seq = gp_ref.shape[1]
    z = gp_ref[0] + b_ref[...]
    log_f = jnp.minimum(z, 0.0) - _log1p_exp_neg_abs(z)
    r = lax.broadcasted_iota(jnp.int32, (blk, blk), 0)
    c = lax.broadcasted_iota(jnp.int32, (blk, blk), 1)
    tri = jnp.where(c <= r, 1.0, 0.0).astype(_BF16)
    carry = jnp.zeros((1, log_f.shape[1]), _F32)
    for s in range(seq // blk):
        part = log_f[s * blk:(s + 1) * blk]
        cs = carry
        for p in _split3(part):
            cs = cs + jnp.dot(tri, p, preferred_element_type=_F32)
        col_ref[0, s * blk:(s + 1) * blk, :] = cs
        carry = cs[blk - 1:blk, :]
    row_ref[0] = jnp.transpose(col_ref[0])[:row_ref.shape[1], :]


def _gates(gate_pre, b_forget, batch, seq):
    ng = gate_pre.shape[1]
    nh = b_forget.shape[0]
    b_pad = jnp.zeros((1, ng), _F32).at[0, :nh].set(b_forget)
    return pl.pallas_call(
        functools.partial(_gates_kernel, blk=ATT_BLOCK),
        grid=(batch,),
        in_specs=[pl.BlockSpec((1, seq, ng), lambda b: (b, 0, 0)),
                  pl.BlockSpec((1, ng), lambda b: (0, 0))],
        out_specs=(pl.BlockSpec((1, seq, ng), lambda b: (b, 0, 0)),
                   pl.BlockSpec((1, nh, seq), lambda b: (b, 0, 0))),
        out_shape=(jax.ShapeDtypeStruct((batch, seq, ng), _F32),
                   jax.ShapeDtypeStruct((batch, nh, seq), _F32)),
        compiler_params=pltpu.CompilerParams(dimension_semantics=("parallel",)),
        name="gates",
    )(gate_pre.reshape(batch, seq, ng), b_pad)


def _transpose_bf16(x):
    return jnp.transpose(x)


def _block_iotas(blk):
    key = lax.broadcasted_iota(jnp.int32, (blk, blk), 0)
    qry = lax.broadcasted_iota(jnp.int32, (blk, blk), 1)
    return key, qry


def _rows(ref, j, blk):
    return ref[pl.ds(pl.multiple_of(j * blk, blk), blk), :]


def _cols(ref, j, blk):
    return ref[:, pl.ds(pl.multiple_of(j * blk, blk), blk)]


def _chain_sweep(logits, finish, states, chains, base, n_shared):
    def wave(w):
        return [(c, base + c - w, w, w == 0) for c in range(w, chains)]

    def shared(step):
        j = jnp.maximum(base - 1 - step, 0)
        return [(c, j, c + 1 + step, False) for c in range(chains)]

    states = list(states)
    us = logits(wave(0))
    for w in range(chains):
        if w + 1 < chains:
            ahead = logits(wave(w + 1))
        elif n_shared is not None:
            ahead = logits(shared(0))
        new = finish(wave(w), us, states[w:])
        states[w:] = new
        us = ahead
    if n_shared is None:
        return states

    def body(step, carry):
        us, sts = carry
        ahead = logits(shared(step + 1))
        return tuple(ahead), tuple(finish(shared(step), list(us), list(sts)))

    return lax.fori_loop(0, n_shared, body, (tuple(us), tuple(states)))[1]


def _sb_head(qt, k_ref, vt_ref, o_ref, tile, blk, chains, single_tile):
    scale2 = HEAD_DIM ** -0.5 * LOG2_E
    key, qry = _block_iotas(blk)
    strict = key < qry
    suffix = jnp.where(qry > key, 1.0, 0.0).astype(_BF16)

    def gates(z2, diagonal):
        log_beta = jnp.minimum(z2, 0.0) - jnp.log2(1.0 + jnp.exp2(-jnp.abs(z2)))
        log_not_beta = log_beta - z2
        if diagonal:
            log_not_beta = jnp.where(strict, log_not_beta, 0.0)
        hi = log_not_beta.astype(_BF16)
        lo = (log_not_beta - hi.astype(_F32)).astype(_BF16)
        return log_beta, log_not_beta, hi, lo

    def logits(items):
        return [jnp.dot(_rows(k_ref, j, blk), qt[c], preferred_element_type=_F32) * scale2
                for c, j, _, _ in items]

    def finish(items, z2, states):
        g = [gates(z, diag) for z, (_, _, _, diag) in zip(z2, items)]
        after = [jnp.dot(suffix, hi, preferred_element_type=_F32)
                 + jnp.dot(suffix, lo, preferred_element_type=_F32) + carry
                 for (_, _, hi, lo), (carry, _) in zip(g, states)]
        new = []
        for (log_beta, log_not_beta, _, _), aft, (carry, acc), (_, j, _, diag) in zip(
                g, after, states, items):
            w = jnp.exp2(log_beta + aft)
            if diag:
                w = jnp.where(strict, w, 0.0)
            acc = acc + jnp.dot(_cols(vt_ref, j, blk), w.astype(_BF16),
                                preferred_element_type=_F32)
            new.append((carry + jnp.sum(log_not_beta, axis=0, keepdims=True), acc))
        return new

    init = (jnp.zeros((1, blk), _F32), jnp.zeros((HEAD_DIM, blk), _F32))
    base = tile * chains
    states = _chain_sweep(logits, finish, [init] * chains, chains, base,
                          None if single_tile else base)
    for c in range(chains):
        o_ref[c * blk:(c + 1) * blk, :] = jnp.transpose(states[c][1].astype(o_ref.dtype))


def _softmax_finish(values, shifts=None):
    def finish(items, us, states):
        mids = []
        for u, (c, _, _, _), (m, l, _) in zip(us, items, states):
            u_max = jnp.max(u, axis=0, keepdims=True)
            shift = None if shifts is None else shifts[c]
            if shift is not None:
                u_max = u_max + shift
            m_new = jnp.maximum(m, u_max)
            alpha = jnp.exp2(m - m_new)
            p = jnp.exp2(u - m_new if shift is None else u + (shift - m_new))
            mids.append((m_new, alpha, alpha * l + jnp.sum(p, axis=0, keepdims=True),
                         p.astype(_BF16)))
        return [(m_new, l_new, alpha * acc + jnp.dot(values(j), p, preferred_element_type=_F32))
                for (m_new, alpha, l_new, p), (_, j, _, _), (_, _, acc) in zip(mids, items, states)]
    return finish


def _softmax_init(blk):
    return (jnp.full((1, blk), NEG_INF, _F32), jnp.zeros((1, blk), _F32),
            jnp.zeros((HEAD_DIM, blk), _F32))


def _softmax_store(o_ref, states, blk):
    for c, (_, l, acc) in enumerate(states):
        o_ref[c * blk:(c + 1) * blk, :] = jnp.transpose((acc / l).astype(o_ref.dtype))


FOX_BIAS_PARTS = 3


def _fox_augment_keys(k_ref, cfc_ref, hh, kaug_ref):
    seq = k_ref.shape[0]
    lane = lax.broadcasted_iota(jnp.int32, (seq, cfc_ref.shape[2]), 1)
    cf = jnp.sum(jnp.where(lane == hh, cfc_ref[0], 0.0), axis=1, keepdims=True)
    parts = _split3(cf * -(HEAD_DIM ** 0.5))
    lane = lax.broadcasted_iota(jnp.int32, (seq, HEAD_DIM), 1)
    extra = jnp.zeros((seq, HEAD_DIM), _F32)
    for i, part in enumerate(parts):
        extra = jnp.where(lane == i, part.astype(_F32), extra)
    kaug_ref[:, :HEAD_DIM] = k_ref[...]
    kaug_ref[:, HEAD_DIM:] = extra.astype(_BF16)


def _fox_head(qt, kaug_ref, vt_ref, cfr_ref, o_ref, hh, tile, blk, chains, single_tile):
    scale2 = HEAD_DIM ** -0.5 * LOG2_E
    key, qry = _block_iotas(blk)
    causal = key <= qry
    sub = lax.broadcasted_iota(jnp.int32, (HEAD_DIM, blk), 0)
    ones_rows = jnp.where(sub < FOX_BIAS_PARTS, 1.0, 0.0).astype(_BF16)
    base = tile * chains
    qt_aug = [jnp.concatenate([qt[c], ones_rows], axis=0) for c in range(chains)]
    cf_q = [cfr_ref[0, hh, pl.ds(base + c, 1), :] * LOG2_E for c in range(chains)]

    def logit(c, j, dist, diagonal):
        u = jnp.dot(_rows(kaug_ref, j, blk), qt_aug[c], preferred_element_type=_F32) * scale2
        return jnp.where(causal, u, NEG_INF) if diagonal else u

    states = _chain_sweep(lambda items: [logit(*it) for it in items],
                          _softmax_finish(lambda j: _cols(vt_ref, j, blk), cf_q),
                          [_softmax_init(blk)] * chains, chains, base,
                          None if single_tile else base)
    _softmax_store(o_ref, states, blk)


def _attn_even_kernel(q_ref, k_ref, v_ref, cfc_ref, cfr_ref, o_ref, vt_ref, kaug_ref,
                      *, blk, chains, single_tile):
    h = pl.program_id(1)
    tile = pl.program_id(2)

    @pl.when(tile == 0)
    def _():
        vt_ref[...] = _transpose_bf16(v_ref[...])

    @pl.when((tile == 0) & (h >= N_SB_HEADS))
    def _():
        _fox_augment_keys(k_ref, cfc_ref, h - N_SB_HEADS, kaug_ref)

    qt_all = _transpose_bf16(q_ref[...])
    qt = [qt_all[:, c * blk:(c + 1) * blk] for c in range(chains)]

    @pl.when(h < N_SB_HEADS)
    def _():
        _sb_head(qt, k_ref, vt_ref, o_ref, tile, blk, chains, single_tile)

    @pl.when(h >= N_SB_HEADS)
    def _():
        _fox_head(qt, kaug_ref, vt_ref, cfr_ref, o_ref, h - N_SB_HEADS, tile, blk, chains,
                  single_tile)


def _attn_specs(batch, seq, n_heads, rows):
    nq = seq // rows
    q_spec = pl.BlockSpec((None, rows, HEAD_DIM), lambda b, h, i: (h, b * nq + i, 0))
    k_spec = pl.BlockSpec((None, seq, HEAD_DIM), lambda b, h, i: (n_heads + h, b, 0))
    v_spec = pl.BlockSpec((None, seq, HEAD_DIM), lambda b, h, i: (2 * n_heads + h, b, 0))
    o_spec = pl.BlockSpec((rows, HEAD_DIM), lambda b, h, i: (b * nq + i, h))
    return nq, q_spec, k_spec, v_spec, o_spec


def _attn_even(qkv, cf_col, cf_row, batch, seq, *, blk=ATT_BLOCK, chains=ATT_CHAINS):
    n_heads = N_SB_HEADS + N_FOX_HEADS
    nq, q_spec, k_spec, v_spec, o_spec = _attn_specs(batch, seq, n_heads, blk * chains)
    ng = cf_col.shape[2]
    nk = seq // blk
    cf_row = cf_row.reshape(batch, N_FOX_HEADS, nk, blk)
    return pl.pallas_call(
        functools.partial(_attn_even_kernel, blk=blk, chains=chains, single_tile=nq == 1),
        grid=(batch, n_heads, nq),
        in_specs=[q_spec, k_spec, v_spec,
                  pl.BlockSpec((1, seq, ng), lambda b, h, i: (b, 0, 0)),
                  pl.BlockSpec((1, N_FOX_HEADS, nk, blk), lambda b, h, i: (b, 0, 0, 0))],
        out_specs=o_spec,
        out_shape=jax.ShapeDtypeStruct((batch * seq, n_heads * HEAD_DIM), _BF16),
        scratch_shapes=[pltpu.VMEM((HEAD_DIM, seq), _BF16),
                        pltpu.VMEM((seq, 2 * HEAD_DIM), _BF16)],
        compiler_params=pltpu.CompilerParams(
            dimension_semantics=("parallel", "parallel", "arbitrary")),
        name="attn_even",
    )(qkv, qkv, qkv, cf_col, cf_row)


def _dilated_bias_tables(blk):
    n_near = (DIL_W2 + blk - 1) // blk + 1
    assert blk % DIL_D3 == 0 and n_near * blk - (blk - 1) > DIL_W2
    d = jnp.arange(n_near + 1, dtype=jnp.int32)[:, None, None]
    s = jnp.arange(blk, dtype=jnp.int32)[None, :, None]
    t = jnp.arange(blk, dtype=jnp.int32)[None, None, :]
    delta = d * blk + t - s
    count = ((delta <= DIL_W1).astype(_F32)
             + ((delta % DIL_D2 == 0) & (delta <= DIL_W2)).astype(_F32)
             + (delta % DIL_D3 == 0).astype(_F32))
    count = jnp.where(delta >= 0, count, 0.0)
    return jnp.where(count > 0, jnp.log2(jnp.maximum(count, 1.0)), NEG_INF), n_near


def _attn_dilated_kernel(q_ref, k_ref, v_ref, bias_ref, o_ref, vt_ref, *, blk, chains,
                         n_near, single_tile):
    tile = pl.program_id(2)

    @pl.when(tile == 0)
    def _():
        vt_ref[...] = _transpose_bf16(v_ref[...])

    scale2 = HEAD_DIM ** -0.5 * LOG2_E
    qt_all = _transpose_bf16(q_ref[...])
    qt = [qt_all[:, c * blk:(c + 1) * blk] for c in range(chains)]
    base = tile * chains

    def logit(c, j, dist, diagonal):
        table = min(dist, n_near) if isinstance(dist, int) else jnp.minimum(dist, n_near)
        return (jnp.dot(_rows(k_ref, j, blk), qt[c], preferred_element_type=_F32) * scale2
                + bias_ref[table])

    states = _chain_sweep(lambda items: [logit(*it) for it in items],
                          _softmax_finish(lambda j: _cols(vt_ref, j, blk)),
                          [_softmax_init(blk)] * chains, chains, base,
                          None if single_tile else base)
    _softmax_store(o_ref, states, blk)


def _attn_dilated(qkv, batch, seq, n_heads, *, blk=ATT_BLOCK, chains=ATT_CHAINS):
    assert seq <= DIL_D3 * DIL_W1
    nq, q_spec, k_spec, v_spec, o_spec = _attn_specs(batch, seq, n_heads, blk * chains)
    bias, n_near = _dilated_bias_tables(blk)
    return pl.pallas_call(
        functools.partial(_attn_dilated_kernel, blk=blk, chains=chains, n_near=n_near,
                          single_tile=nq == 1),
        grid=(batch, n_heads, nq),
        in_specs=[q_spec, k_spec, v_spec,
                  pl.BlockSpec(bias.shape, lambda b, h, i: (0, 0, 0))],
        out_specs=o_spec,
        out_shape=jax.ShapeDtypeStruct((batch * seq, n_heads * HEAD_DIM), _BF16),
        scratch_shapes=[pltpu.VMEM((HEAD_DIM, seq), _BF16)],
        compiler_params=pltpu.CompilerParams(
            dimension_semantics=("parallel", "parallel", "arbitrary")),
        name="attn_dilated",
    )(qkv, qkv, qkv, bias)


def _outproj_kernel(o_ref, w_ref, x_ref, y_ref):
    y_ref[...] = x_ref[...] + jnp.dot(o_ref[...], w_ref[...].astype(_BF16),
                                      preferred_element_type=_F32)


def _outproj(o, w, layer, x, *, tm=1024, tn=512):
    m, d_in = o.shape
    d = w.shape[2]
    est = 2 * tm * d_in * 2 + 2 * d_in * tn * 4 + d_in * tn * 2 + 4 * tm * tn * 4
    return pl.pallas_call(
        _outproj_kernel,
        grid=(m // tm, d // tn),
        in_specs=[pl.BlockSpec((tm, d_in), lambda i, j: (i, 0)),
                  pl.BlockSpec((None, d_in, tn), lambda i, j: (layer, 0, j)),
                  pl.BlockSpec((tm, tn), lambda i, j: (i, j))],
        out_specs=pl.BlockSpec((tm, tn), lambda i, j: (i, j)),
        out_shape=jax.ShapeDtypeStruct((m, d), _F32),
        compiler_params=pltpu.CompilerParams(
            dimension_semantics=("parallel", "parallel"),
            vmem_limit_bytes=_vmem_limit(est)),
        name="outproj",
    )(o, w, x)


def kernel(x, norm_g, ffn1_w_gate, ffn1_w_up, ffn1_w_down, ffn2_w_gate, ffn2_w_up,
           ffn2_w_down, even_w_in, even_b_forget, even_w_out, odd_w_qkv, odd_w_out,
           final_norm_g):
    batch, seq, d = x.shape
    depth = norm_g.shape[0]
    n_heads = d // HEAD_DIM
    assert n_heads == N_SB_HEADS + N_FOX_HEADS
    xf = x.reshape(batch * seq, d)
    for layer in range(depth):
        j = layer // 2
        xf = _ffn(xf, norm_g[layer, 0], ffn1_w_gate, ffn1_w_up, ffn1_w_down, layer)
        if layer % 2 == 0:
            w_gate = jnp.zeros((d, V7X_LANES), _BF16).at[:, :N_FOX_HEADS].set(
                even_w_in[j, :, 3 * d:].astype(_BF16))
            qkv, gate_pre = _proj(xf, norm_g[layer, 1], even_w_in, j, 3 * d, w_gate=w_gate)
            cf_col, cf_row = _gates(gate_pre, even_b_forget[j], batch, seq)
            o = _attn_even(qkv, cf_col, cf_row, batch, seq)
            xf = _outproj(o, even_w_out, j, xf)
        else:
            qkv = _proj(xf, norm_g[layer, 1], odd_w_qkv, j, 3 * d, rope_seq=seq,
                        rope_cols=2 * d)
            o = _attn_dilated(qkv, batch, seq, n_heads)
            xf = _outproj(o, odd_w_out, j, xf)
        last = layer == depth - 1
        xf = _ffn(xf, norm_g[layer, 2], ffn2_w_gate, ffn2_w_up, ffn2_w_down, layer,
                  final_norm_g if last else None)
    return xf.reshape(batch, seq, d)
```

```python
import functools

import jax
import jax.numpy as jnp
from jax import lax
from jax.experimental import pallas as pl
from jax.experimental.pallas import tpu as pltpu

_F32 = jnp.float32
_BF16 = jnp.bfloat16

HEAD_DIM = 128
N_SB_HEADS = 8
N_FOX_HEADS = 8
ROPE_DIMS = HEAD_DIM // 4
ROPE_THETA = 500000.0
RMS_EPS = 1e-6
NEG_INF = -1e30
LOG2_E = 1.4426950408889634
DIL_W1, DIL_W2, DIL_D2, DIL_D3 = 128, 512, 4, 16

V7X_LANES = 128
V7X_VMEM_BYTES = 64 * 1024 * 1024
V7X_SCOPED_VMEM_DEFAULT = 32 * 1024 * 1024
VMEM_CAP_BYTES = V7X_VMEM_BYTES * 15 // 16

FFN_RING = 3
FFN_REUSE = 2
PROJ_CHUNK = 256
ATT_BLOCK = 256
ATT_CHAINS = 8
ATT_HEAD_GROUP = 4


def _vmem_limit(block_bytes):
    return int(min(VMEM_CAP_BYTES, max(V7X_SCOPED_VMEM_DEFAULT, block_bytes * 5 // 4)))


def _rms(x, g):
    ms = jnp.mean(x * x, axis=-1, keepdims=True)
    return x * lax.rsqrt(ms + RMS_EPS) * g


def _log1p_exp_neg_abs(z):
    return jnp.log(1.0 + jnp.exp(-jnp.abs(z)))


def _ffn_kernel(*refs, final_norm, layer, n_f, tf):
    if final_norm:
        (x_ref, g_ref, wg_hbm, wu_hbm, wd_hbm, gf_ref, o_ref,
         n_ref, wg_buf, wu_buf, wd_buf, sem) = refs
    else:
        x_ref, g_ref, wg_hbm, wu_hbm, wd_hbm, o_ref, n_ref, wg_buf, wu_buf, wd_buf, sem = refs

    tile = pl.program_id(0)
    fresh = n_f - FFN_REUSE
    first = jnp.where(tile == 0, 0, n_f + (tile - 1) * fresh)
    total = n_f + (pl.num_programs(0) - 1) * fresh

    def weight_copies(f):
        later = jnp.maximum(f - n_f, 0)
        t = jnp.where(f < n_f, 0, 1 + later // fresh)
        s = jnp.where(f < n_f, f, FFN_REUSE + later % fresh)
        block = jnp.where(t % 2 == 0, s, n_f - 1 - s)
        slot = f % FFN_RING
        cols = pl.ds(pl.multiple_of(block * tf, tf), tf)
        return (pltpu.make_async_copy(wg_hbm.at[layer, :, cols], wg_buf.at[slot], sem.at[0, slot]),
                pltpu.make_async_copy(wu_hbm.at[layer, :, cols], wu_buf.at[slot], sem.at[1, slot]),
                pltpu.make_async_copy(wd_hbm.at[layer, cols, :], wd_buf.at[slot], sem.at[2, slot]))

    def start(f):
        @pl.when(f < total)
        def _():
            for copy in weight_copies(f):
                copy.start()

    @pl.when(tile == 0)
    def _():
        for f in range(FFN_RING - 1):
            start(jnp.int32(f))

    x = x_ref[...]
    n_ref[...] = _rms(x, g_ref[...]).astype(_BF16)
    o_ref[...] = x

    def step(s, carry):
        reused = (tile > 0) & (s < FFN_REUSE)
        f = jnp.where(reused, first - 1 - s, first + s - jnp.where(tile == 0, 0, FFN_REUSE))
        slot = f % FFN_RING
        ahead = jnp.where(tile == 0, s + 2,
                          jnp.where(s == 1, first + 2, jnp.where(s == 2, first + 1, first + s)))
        can_start = (s <= n_f - 2) & ((tile == 0) | (s >= 1))

        @pl.when(can_start)
        def _():
            start(ahead)

        @pl.when(jnp.logical_not(reused))
        def _():
            for copy in weight_copies(f):
                copy.wait()

        n = n_ref[...]
        gate = jnp.dot(n, wg_buf[slot].astype(_BF16), preferred_element_type=_F32)
        up = jnp.dot(n, wu_buf[slot].astype(_BF16), preferred_element_type=_F32)
        h = (gate * jax.nn.sigmoid(gate)) * (up * 0.5)
        o_ref[...] += jnp.dot(h.astype(_BF16), wd_buf[slot].astype(_BF16),
                              preferred_element_type=_F32)
        return carry

    lax.fori_loop(0, n_f, step, 0)
    if final_norm:
        o_ref[...] = _rms(o_ref[...], gf_ref[...])


def _ffn(x, g, wg, wu, wd, layer, final_g=None, *, tm=1024, tf=256):
    m, d = x.shape
    f = wg.shape[2]
    assert m % tm == 0 and f % tf == 0
    final_norm = final_g is not None
    hbm = pl.BlockSpec(memory_space=pl.ANY)
    in_specs = [pl.BlockSpec((tm, d), lambda i: (i, 0)), pl.BlockSpec((1, d), lambda i: (0, 0)),
                hbm, hbm, hbm]
    args = [x, g.reshape(1, d), wg, wu, wd]
    if final_norm:
        in_specs.append(pl.BlockSpec((1, d), lambda i: (0, 0)))
        args.append(final_g.reshape(1, d))
    est = (2 * (2 * tm * d * 4) + tm * d * 2 + FFN_RING * 3 * d * tf * 4 + 3 * d * tf * 2
           + 4 * tm * tf * 4)
    return pl.pallas_call(
        functools.partial(_ffn_kernel, final_norm=final_norm, layer=layer, n_f=f // tf, tf=tf),
        grid=(m // tm,),
        in_specs=in_specs,
        out_specs=pl.BlockSpec((tm, d), lambda i: (i, 0)),
        out_shape=jax.ShapeDtypeStruct((m, d), _F32),
        scratch_shapes=[pltpu.VMEM((tm, d), _BF16),
                        pltpu.VMEM((FFN_RING, d, tf), _F32), pltpu.VMEM((FFN_RING, d, tf), _F32),
                        pltpu.VMEM((FFN_RING, tf, d), _F32),
                        pltpu.SemaphoreType.DMA((3, FFN_RING))],
        compiler_params=pltpu.CompilerParams(
            dimension_semantics=("arbitrary",),
            vmem_limit_bytes=_vmem_limit(est)),
        name="ffn_final" if final_norm else "ffn",
    )(*args)


def _snake(i, j, n):
    return jnp.where(i % 2 == 0, j, n - 1 - j)


def _store_heads(o_ref, col, y):
    for c in range(0, y.shape[1], HEAD_DIM):
        o_ref[(col + c) // HEAD_DIM] = y[:, c:c + HEAD_DIM].astype(o_ref.dtype)


def _proj_gate_kernel(x_ref, g_ref, w_ref, wf_ref, o_ref, gate_ref, n_ref):
    j = pl.program_id(1)

    @pl.when(j == 0)
    def _():
        n = _rms(x_ref[...], g_ref[...]).astype(_BF16)
        n_ref[...] = n
        gate_ref[...] = jnp.dot(n, wf_ref[...], preferred_element_type=_F32)

    for c in range(0, w_ref.shape[1], PROJ_CHUNK):
        y = jnp.dot(n_ref[...], w_ref[:, c:c + PROJ_CHUNK].astype(_BF16),
                    preferred_element_type=_F32)
        _store_heads(o_ref, c, y)


def _proj_rope_kernel(x_ref, g_ref, w_ref, cos_ref, sin_lo_ref, sin_hi_ref, o_ref, n_ref,
                      *, rope_blocks):
    @pl.when(pl.program_id(1) == 0)
    def _():
        n_ref[...] = _rms(x_ref[...], g_ref[...]).astype(_BF16)

    j = _snake(pl.program_id(0), pl.program_id(1), pl.num_programs(1))

    half = ROPE_DIMS // 2
    chunks = range(0, w_ref.shape[1], PROJ_CHUNK)

    def project(col):
        return jnp.dot(n_ref[...], w_ref[:, col:col + PROJ_CHUNK].astype(_BF16),
                       preferred_element_type=_F32)

    @pl.when(j < rope_blocks)
    def _():
        reps = PROJ_CHUNK // HEAD_DIM
        cos, sin_lo, sin_hi = (jnp.tile(t[...], (1, reps))
                               for t in (cos_ref, sin_lo_ref, sin_hi_ref))
        for col in chunks:
            y = project(col)
            _store_heads(o_ref, col, y * cos + pltpu.roll(y, PROJ_CHUNK - half, 1) * sin_lo
                         + pltpu.roll(y, half, 1) * sin_hi)

    @pl.when(j >= rope_blocks)
    def _():
        for col in chunks:
            _store_heads(o_ref, col, project(col))


def _rope_tables(seq):
    half = ROPE_DIMS // 2
    freqs = ROPE_THETA ** (-jnp.arange(half, dtype=_F32) / half)
    ang = jnp.arange(seq, dtype=_F32)[:, None] * freqs[None, :]
    cos, sin = jnp.cos(ang), jnp.sin(ang)
    pad = jnp.zeros((seq, HEAD_DIM - ROPE_DIMS), _F32)
    zero = jnp.zeros((seq, half), _F32)
    cos_t = jnp.concatenate([cos, cos, pad + 1.0], axis=1)
    sin_lo = jnp.concatenate([-sin, zero, pad], axis=1)
    sin_hi = jnp.concatenate([zero, sin, pad], axis=1)
    return cos_t, sin_lo, sin_hi


def _proj(x, g, w, layer, n_out, *, w_gate=None, rope_seq=None, rope_cols=0, tm=1024,
          tn=1024):
    m, d = x.shape
    assert m % tm == 0 and n_out % tn == 0 and n_out <= w.shape[2] and tn % PROJ_CHUNK == 0
    n_j = n_out // tn
    in_specs = [
        pl.BlockSpec((tm, d), lambda i, j: (i, 0)),
        pl.BlockSpec((1, d), lambda i, j: (0, 0)),
        pl.BlockSpec((d, tn), lambda i, j: (layer, _snake(i, j, n_j))),
    ]
    args = [x, g.reshape(1, d), w.reshape(-1, w.shape[2])]
    out_specs = pl.BlockSpec((tn // HEAD_DIM, tm, HEAD_DIM),
                             lambda i, j: (_snake(i, j, n_j), i, 0))
    out_shape = jax.ShapeDtypeStruct((n_out // HEAD_DIM, m, HEAD_DIM), _BF16)
    est = (2 * tm * d * 4 + tm * d * 2 + 2 * d * tn * 4 + d * tn * 2 + 2 * tm * tn * 2
           + 2 * tm * tn * 4)
    if w_gate is not None:
        ng = w_gate.shape[1]
        in_specs.append(pl.BlockSpec((d, ng), lambda i, j: (0, 0)))
        args.append(w_gate)
        out_specs = (out_specs, pl.BlockSpec((tm, ng), lambda i, j: (i, 0)))
        out_shape = (out_shape, jax.ShapeDtypeStruct((m, ng), _F32))
        body = _proj_gate_kernel
        name = "proj_gate"
    else:
        assert rope_seq % tm == 0 and rope_cols % tn == 0
        seq_blocks = rope_seq // tm
        tab_spec = pl.BlockSpec((tm, HEAD_DIM), lambda i, j: (i % seq_blocks, 0))
        in_specs += [tab_spec] * 3
        args += list(_rope_tables(rope_seq))
        est += 3 * 2 * tm * HEAD_DIM * 4
        body = functools.partial(_proj_rope_kernel, rope_blocks=rope_cols // tn)
        name = "proj_rope"
    return pl.pallas_call(
        body,
        grid=(m // tm, n_out // tn),
        in_specs=in_specs,
        out_specs=out_specs,
        out_shape=out_shape,
        scratch_shapes=[pltpu.VMEM((tm, d), _BF16)],
        compiler_params=pltpu.CompilerParams(
            dimension_semantics=("parallel", "arbitrary"),
            vmem_limit_bytes=_vmem_limit(est)),
        name=name,
    )(*args)


def _split3(x):
    hi = x.astype(_BF16)
    r = x - hi.astype(_F32)
    mid = r.astype(_BF16)
    lo = (r - mid.astype(_F32)).astype(_BF16)
    return hi, mid, lo


def _gates_kernel(gp_ref, b_ref, col_ref, row_ref, *, blk):
    seq = gp_ref.shape[1]
    z = gp_ref[0] + b_ref[...]
    log_f = jnp.minimum(z, 0.0) - _log1p_exp_neg_abs(z)
    r = lax.broadcasted_iota(jnp.int32, (blk, blk), 0)
    c = lax.broadcasted_iota(jnp.int32, (blk, blk), 1)
    tri = jnp.where(c <= r, 1.0, 0.0).astype(_BF16)
    carry = jnp.zeros((1, log_f.shape[1]), _F32)
    for s in range(seq // blk):
        part = log_f[s * blk:(s + 1) * blk]
        cs = carry
        for p in _split3(part):
            cs = cs + jnp.dot(tri, p, preferred_element_type=_F32)
        col_ref[0, s * blk:(s + 1) * blk, :] = cs
        carry = cs[blk - 1:blk, :]
    row_ref[0] = jnp.transpose(col_ref[0])[:row_ref.shape[1], :]


def _gates(gate_pre, b_forget, batch, seq):
    ng = gate_pre.shape[1]
    nh = b_forget.shape[0]
    b_pad = jnp.zeros((1, ng), _F32).at[0, :nh].set(b_forget)
    return pl.pallas_call(
        functools.partial(_gates_kernel, blk=ATT_BLOCK),
        grid=(batch,),
        in_specs=[pl.BlockSpec((1, seq, ng), lambda b: (b, 0, 0)),
                  pl.BlockSpec((1, ng), lambda b: (0, 0))],
        out_specs=(pl.BlockSpec((1, seq, ng), lambda b: (b, 0, 0)),
                   pl.BlockSpec((1, nh, seq), lambda b: (b, 0, 0))),
        out_shape=(jax.ShapeDtypeStruct((batch, seq, ng), _F32),
                   jax.ShapeDtypeStruct((batch, nh, seq), _F32)),
        compiler_params=pltpu.CompilerParams(dimension_semantics=("parallel",)),
        name="gates",
    )(gate_pre.reshape(batch, seq, ng), b_pad)


def _transpose_bf16(x):
    return jnp.transpose(x)


def _block_iotas(blk):
    key = lax.broadcasted_iota(jnp.int32, (blk, blk), 0)
    qry = lax.broadcasted_iota(jnp.int32, (blk, blk), 1)
    return key, qry


def _rows(ref, j, blk):
    return ref[pl.ds(pl.multiple_of(j * blk, blk), blk), :]


def _cols(ref, j, blk):
    return ref[:, pl.ds(pl.multiple_of(j * blk, blk), blk)]


def _chain_sweep(logits, finish, states, chains, base, n_shared):
    def wave(w):
        return [(c, base + c - w, w, w == 0) for c in range(w, chains)]

    def shared(step):
        j = jnp.maximum(base - 1 - step, 0)
        return [(c, j, c + 1 + step, False) for c in range(chains)]

    states = list(states)
    us = logits(wave(0))
    for w in range(chains):
        if w + 1 < chains:
            ahead = logits(wave(w + 1))
        elif n_shared is not None:
            ahead = logits(shared(0))
        new = finish(wave(w), us, states[w:])
        states[w:] = new
        us = ahead
    if n_shared is None:
        return states

    def body(step, carry):
        us, sts = carry
        ahead = logits(shared(step + 1))
        return tuple(ahead), tuple(finish(shared(step), list(us), list(sts)))

    return lax.fori_loop(0, n_shared, body, (tuple(us), tuple(states)))[1]


def _sb_head(qt, k_ref, vt_ref, o_ref, tile, blk, chains, single_tile):
    scale2 = HEAD_DIM ** -0.5 * LOG2_E
    key, qry = _block_iotas(blk)
    strict = key < qry
    suffix = jnp.where(qry > key, 1.0, 0.0).astype(_BF16)

    def gates(z2, diagonal):
        log_beta = jnp.minimum(z2, 0.0) - jnp.log2(1.0 + jnp.exp2(-jnp.abs(z2)))
        log_not_beta = log_beta - z2
        if diagonal:
            log_not_beta = jnp.where(strict, log_not_beta, 0.0)
        hi = log_not_beta.astype(_BF16)
        lo = (log_not_beta - hi.astype(_F32)).astype(_BF16)
        return log_beta, log_not_beta, hi, lo

    def logits(items):
        return [jnp.dot(_rows(k_ref, j, blk), qt[c], preferred_element_type=_F32) * scale2
                for c, j, _, _ in items]

    def finish(items, z2, states):
        g = [gates(z, diag) for z, (_, _, _, diag) in zip(z2, items)]
        after = [jnp.dot(suffix, hi, preferred_element_type=_F32)
                 + jnp.dot(suffix, lo, preferred_element_type=_F32) + carry
                 for (_, _, hi, lo), (carry, _) in zip(g, states)]
        new = []
        for (log_beta, log_not_beta, _, _), aft, (carry, acc), (_, j, _, diag) in zip(
                g, after, states, items):
            w = jnp.exp2(log_beta + aft)
            if diag:
                w = jnp.where(strict, w, 0.0)
            acc = acc + jnp.dot(_cols(vt_ref, j, blk), w.astype(_BF16),
                                preferred_element_type=_F32)
            new.append((carry + jnp.sum(log_not_beta, axis=0, keepdims=True), acc))
        return new

    init = (jnp.zeros((1, blk), _F32), jnp.zeros((HEAD_DIM, blk), _F32))
    base = tile * chains
    states = _chain_sweep(logits, finish, [init] * chains, chains, base,
                          None if single_tile else base)
    for c in range(chains):
        o_ref[c * blk:(c + 1) * blk, :] = jnp.transpose(states[c][1].astype(o_ref.dtype))


def _softmax_finish(values, shifts=None):
    def finish(items, us, states):
        mids = []
        for u, (c, _, _, _), (m, l, _) in zip(us, items, states):
            u_max = jnp.max(u, axis=0, keepdims=True)
            shift = None if shifts is None else shifts[c]
            if shift is not None:
                u_max = u_max + shift
            m_new = jnp.maximum(m, u_max)
            alpha = jnp.exp2(m - m_new)
            p = jnp.exp2(u - m_new if shift is None else u + (shift - m_new))
            mids.append((m_new, alpha, alpha * l + jnp.sum(p, axis=0, keepdims=True),
                         p.astype(_BF16)))
        return [(m_new, l_new, alpha * acc + jnp.dot(values(j), p, preferred_element_type=_F32))
                for (m_new, alpha, l_new, p), (_, j, _, _), (_, _, acc) in zip(mids, items, states)]
    return finish


def _softmax_init(blk):
    return (jnp.full((1, blk), NEG_INF, _F32), jnp.zeros((1, blk), _F32),
            jnp.zeros((HEAD_DIM, blk), _F32))


def _softmax_store(o_ref, states, blk):
    for c, (_, l, acc) in enumerate(states):
        o_ref[c * blk:(c + 1) * blk, :] = jnp.transpose((acc / l).astype(o_ref.dtype))


FOX_BIAS_PARTS = 3


def _fox_augment_keys(k_ref, cfc_ref, hh, kaug_ref):
    seq = k_ref.shape[0]
    lane = lax.broadcasted_iota(jnp.int32, (seq, cfc_ref.shape[2]), 1)
    cf = jnp.sum(jnp.where(lane == hh, cfc_ref[0], 0.0), axis=1, keepdims=True)
    parts = _split3(cf * -(HEAD_DIM ** 0.5))
    lane = lax.broadcasted_iota(jnp.int32, (seq, HEAD_DIM), 1)
    extra = jnp.zeros((seq, HEAD_DIM), _F32)
    for i, part in enumerate(parts):
        extra = jnp.where(lane == i, part.astype(_F32), extra)
    kaug_ref[:, :HEAD_DIM] = k_ref[...]
    kaug_ref[:, HEAD_DIM:] = extra.astype(_BF16)


def _fox_head(qt, kaug_ref, vt_ref, cfr_ref, o_ref, hh, tile, blk, chains, single_tile):
    scale2 = HEAD_DIM ** -0.5 * LOG2_E
    key, qry = _block_iotas(blk)
    causal = key <= qry
    sub = lax.broadcasted_iota(jnp.int32, (HEAD_DIM, blk), 0)
    ones_rows = jnp.where(sub < FOX_BIAS_PARTS, 1.0, 0.0).astype(_BF16)
    base = tile * chains
    qt_aug = [jnp.concatenate([qt[c], ones_rows], axis=0) for c in range(chains)]
    cf_q = [cfr_ref[0, hh, pl.ds(base + c, 1), :] * LOG2_E for c in range(chains)]

    def logit(c, j, dist, diagonal):
        u = jnp.dot(_rows(kaug_ref, j, blk), qt_aug[c], preferred_element_type=_F32) * scale2
        return jnp.where(causal, u, NEG_INF) if diagonal else u

    states = _chain_sweep(lambda items: [logit(*it) for it in items],
                          _softmax_finish(lambda j: _cols(vt_ref, j, blk), cf_q),
                          [_softmax_init(blk)] * chains, chains, base,
                          None if single_tile else base)
    _softmax_store(o_ref, states, blk)


def _head_refs(gi, q_ref, k_ref, v_ref, o_ref):
    cols = pl.ds(pl.multiple_of(gi * HEAD_DIM, HEAD_DIM), HEAD_DIM)
    return q_ref.at[gi], k_ref.at[gi], v_ref.at[gi], o_ref.at[:, cols]


def _query_chains(q, blk, chains):
    qt_all = _transpose_bf16(q[...])
    return [qt_all[:, c * blk:(c + 1) * blk] for c in range(chains)]


def _attn_even_kernel(q_ref, k_ref, v_ref, cfc_ref, cfr_ref, o_ref, vt_ref, kaug_ref,
                      *, blk, chains, single_tile, group):
    first_head = pl.program_id(1) * group
    tile = pl.program_id(2)

    def head(gi, stick_breaking):
        q, k, v, o = _head_refs(gi, q_ref, k_ref, v_ref, o_ref)
        vt_ref[...] = _transpose_bf16(v[...])
        qt = _query_chains(q, blk, chains)
        if stick_breaking:
            _sb_head(qt, k, vt_ref, o, tile, blk, chains, single_tile)
        else:
            hh = first_head + gi - N_SB_HEADS
            _fox_augment_keys(k, cfc_ref, hh, kaug_ref)
            _fox_head(qt, kaug_ref, vt_ref, cfr_ref, o, hh, tile, blk, chains, single_tile)

    def heads(stick_breaking):
        def body(gi, carry):
            head(gi, stick_breaking)
            return carry
        lax.fori_loop(0, group, body, 0)

    pl.when(first_head < N_SB_HEADS)(functools.partial(heads, True))
    pl.when(first_head >= N_SB_HEADS)(functools.partial(heads, False))


def _attn_specs(batch, seq, n_heads, rows, group):
    assert n_heads % group == 0
    nq = seq // rows
    ng = n_heads // group
    q_spec = pl.BlockSpec((group, rows, HEAD_DIM), lambda b, h, i: (h, b * nq + i, 0))
    k_spec = pl.BlockSpec((group, seq, HEAD_DIM), lambda b, h, i: (ng + h, b, 0))
    v_spec = pl.BlockSpec((group, seq, HEAD_DIM), lambda b, h, i: (2 * ng + h, b, 0))
    o_spec = pl.BlockSpec((rows, group * HEAD_DIM), lambda b, h, i: (b * nq + i, h))
    return nq, ng, q_spec, k_spec, v_spec, o_spec


def _attn_even(qkv, cf_col, cf_row, batch, seq, *, blk=ATT_BLOCK, chains=ATT_CHAINS,
               group=ATT_HEAD_GROUP):
    n_heads = N_SB_HEADS + N_FOX_HEADS
    assert N_SB_HEADS % group == 0
    nq, ng, q_spec, k_spec, v_spec, o_spec = _attn_specs(batch, seq, n_heads, blk * chains, group)
    gate_lanes = cf_col.shape[2]
    nk = seq // blk
    cf_row = cf_row.reshape(batch, N_FOX_HEADS, nk, blk)
    return pl.pallas_call(
        functools.partial(_attn_even_kernel, blk=blk, chains=chains, single_tile=nq == 1,
                          group=group),
        grid=(batch, ng, nq),
        in_specs=[q_spec, k_spec, v_spec,
                  pl.BlockSpec((1, seq, gate_lanes), lambda b, h, i: (b, 0, 0)),
                  pl.BlockSpec((1, N_FOX_HEADS, nk, blk), lambda b, h, i: (b, 0, 0, 0))],
        out_specs=o_spec,
        out_shape=jax.ShapeDtypeStruct((batch * seq, n_heads * HEAD_DIM), _BF16),
        scratch_shapes=[pltpu.VMEM((HEAD_DIM, seq), _BF16),
                        pltpu.VMEM((seq, 2 * HEAD_DIM), _BF16)],
        compiler_params=pltpu.CompilerParams(
            dimension_semantics=("parallel", "parallel", "arbitrary")),
        name="attn_even",
    )(qkv, qkv, qkv, cf_col, cf_row)


def _dilated_bias_tables(blk):
    n_near = (DIL_W2 + blk - 1) // blk + 1
    assert blk % DIL_D3 == 0 and n_near * blk - (blk - 1) > DIL_W2
    d = jnp.arange(n_near + 1, dtype=jnp.int32)[:, None, None]
    s = jnp.arange(blk, dtype=jnp.int32)[None, :, None]
    t = jnp.arange(blk, dtype=jnp.int32)[None, None, :]
    delta = d * blk + t - s
    count = ((delta <= DIL_W1).astype(_F32)
             + ((delta % DIL_D2 == 0) & (delta <= DIL_W2)).astype(_F32)
             + (delta % DIL_D3 == 0).astype(_F32))
    count = jnp.where(delta >= 0, count, 0.0)
    return jnp.where(count > 0, jnp.log2(jnp.maximum(count, 1.0)), NEG_INF), n_near


def _attn_dilated_kernel(q_ref, k_ref, v_ref, bias_ref, o_ref, vt_ref, *, blk, chains,
                         n_near, single_tile, group):
    tile = pl.program_id(2)
    scale2 = HEAD_DIM ** -0.5 * LOG2_E
    base = tile * chains

    def head(gi, carry):
        q, k, v, o = _head_refs(gi, q_ref, k_ref, v_ref, o_ref)
        vt_ref[...] = _transpose_bf16(v[...])
        qt = _query_chains(q, blk, chains)

        def logit(c, j, dist, diagonal):
            table = min(dist, n_near) if isinstance(dist, int) else jnp.minimum(dist, n_near)
            return (jnp.dot(_rows(k, j, blk), qt[c], preferred_element_type=_F32) * scale2
                    + bias_ref[table])

        states = _chain_sweep(lambda items: [logit(*it) for it in items],
                              _softmax_finish(lambda j: _cols(vt_ref, j, blk)),
                              [_softmax_init(blk)] * chains, chains, base,
                              None if single_tile else base)
        _softmax_store(o, states, blk)
        return carry

    lax.fori_loop(0, group, head, 0)


def _attn_dilated(qkv, batch, seq, n_heads, *, blk=ATT_BLOCK, chains=ATT_CHAINS,
                  group=ATT_HEAD_GROUP):
    assert seq <= DIL_D3 * DIL_W1
    nq, ng, q_spec, k_spec, v_spec, o_spec = _attn_specs(batch, seq, n_heads, blk * chains, group)
    bias, n_near = _dilated_bias_tables(blk)
    return pl.pallas_call(
        functools.partial(_attn_dilated_kernel, blk=blk, chains=chains, n_near=n_near,
                          single_tile=nq == 1, group=group),
        grid=(batch, ng, nq),
        in_specs=[q_spec, k_spec, v_spec,
                  pl.BlockSpec(bias.shape, lambda b, h, i: (0, 0, 0))],
        out_specs=o_spec,
        out_shape=jax.ShapeDtypeStruct((batch * seq, n_heads * HEAD_DIM), _BF16),
        scratch_shapes=[pltpu.VMEM((HEAD_DIM, seq), _BF16)],
        compiler_params=pltpu.CompilerParams(
            dimension_semantics=("parallel", "parallel", "arbitrary")),
        name="attn_dilated",
    )(qkv, qkv, qkv, bias)


def _outproj_kernel(o_ref, w_ref, x_ref, y_ref):
    y_ref[...] = x_ref[...] + jnp.dot(o_ref[...], w_ref[...].astype(_BF16),
                                      preferred_element_type=_F32)


def _outproj(o, w, layer, x, *, tm=1024, tn=512):
    m, d_in = o.shape
    d = w.shape[2]
    n_j = d // tn
    est = 2 * tm * d_in * 2 + 2 * d_in * tn * 4 + d_in * tn * 2 + 4 * tm * tn * 4
    return pl.pallas_call(
        _outproj_kernel,
        grid=(m // tm, d // tn),
        in_specs=[pl.BlockSpec((tm, d_in), lambda i, j: (i, 0)),
                  pl.BlockSpec((None, d_in, tn), lambda i, j: (layer, 0, _snake(i, j, n_j))),
                  pl.BlockSpec((tm, tn), lambda i, j: (i, _snake(i, j, n_j)))],
        out_specs=pl.BlockSpec((tm, tn), lambda i, j: (i, _snake(i, j, n_j))),
        out_shape=jax.ShapeDtypeStruct((m, d), _F32),
        compiler_params=pltpu.CompilerParams(
            dimension_semantics=("parallel", "parallel"),
            vmem_limit_bytes=_vmem_limit(est)),
        name="outproj",
    )(o, w, x)


def kernel(x, norm_g, ffn1_w_gate, ffn1_w_up, ffn1_w_down, ffn2_w_gate, ffn2_w_up,
           ffn2_w_down, even_w_in, even_b_forget, even_w_out, odd_w_qkv, odd_w_out,
           final_norm_g):
    batch, seq, d = x.shape
    depth = norm_g.shape[0]
    n_heads = d // HEAD_DIM
    assert n_heads == N_SB_HEADS + N_FOX_HEADS
    xf = x.reshape(batch * seq, d)
    for layer in range(depth):
        j = layer // 2
        xf = _ffn(xf, norm_g[layer, 0], ffn1_w_gate, ffn1_w_up, ffn1_w_down, layer)
        if layer % 2 == 0:
            w_gate = jnp.zeros((d, V7X_LANES), _BF16).at[:, :N_FOX_HEADS].set(
                even_w_in[j, :, 3 * d:].astype(_BF16))
            qkv, gate_pre = _proj(xf, norm_g[layer, 1], even_w_in, j, 3 * d, w_gate=w_gate)
            cf_col, cf_row = _gates(gate_pre, even_b_forget[j], batch, seq)
            o = _attn_even(qkv, cf_col, cf_row, batch, seq)
            xf = _outproj(o, even_w_out, j, xf)
        else:
            qkv = _proj(xf, norm_g[layer, 1], odd_w_qkv, j, 3 * d, rope_seq=seq,
                        rope_cols=2 * d)
            o = _attn_dilated(qkv, batch, seq, n_heads)
            xf = _outproj(o, odd_w_out, j, xf)
        last = layer == depth - 1
        xf = _ffn(xf, norm_g[layer, 2], ffn2_w_gate, ffn2_w_up, ffn2_w_down, layer,
                  final_norm_g if last else None)
    return xf.reshape(batch, seq, d)
```

```python
import functools

import jax
import jax.numpy as jnp
from jax import lax
from jax.experimental import pallas as pl
from jax.experimental.pallas import tpu as pltpu

_F32 = jnp.float32
_BF16 = jnp.bfloat16

HEAD_DIM = 128
N_SB_HEADS = 8
N_FOX_HEADS = 8
ROPE_DIMS = HEAD_DIM // 4
ROPE_THETA = 500000.0
RMS_EPS = 1e-6
NEG_INF = -1e30
LOG2_E = 1.4426950408889634
DIL_W1, DIL_W2, DIL_D2, DIL_D3 = 128, 512, 4, 16

V7X_LANES = 128
V7X_VMEM_BYTES = 64 * 1024 * 1024
V7X_SCOPED_VMEM_DEFAULT = 32 * 1024 * 1024
VMEM_CAP_BYTES = V7X_VMEM_BYTES * 15 // 16

FFN_RING = 3
FFN_AHEAD = FFN_RING - 1
PROJ_CHUNK = 256
ATT_BLOCK = 256
ATT_CHAINS = 8
ATT_HEAD_GROUP = 4


def _vmem_limit(block_bytes):
    return int(min(VMEM_CAP_BYTES, max(V7X_SCOPED_VMEM_DEFAULT, block_bytes * 5 // 4)))


def _rms(x, g):
    ms = jnp.mean(x * x, axis=-1, keepdims=True)
    return x * lax.rsqrt(ms + RMS_EPS) * g


def _log1p_exp_neg_abs(z):
    return jnp.log(1.0 + jnp.exp(-jnp.abs(z)))


def _ffn_kernel(*refs, final_norm, layer, n_f, tf):
    if final_norm:
        (x_ref, g_ref, wg_hbm, wu_hbm, wd_hbm, gf_ref, o_ref,
         n_ref, wg_buf, wu_buf, wd_buf, sem) = refs
    else:
        x_ref, g_ref, wg_hbm, wu_hbm, wd_hbm, o_ref, n_ref, wg_buf, wu_buf, wd_buf, sem = refs

    first = pl.program_id(0) * n_f
    total = pl.num_programs(0) * n_f

    def weight_copies(g):
        slot = g % FFN_RING
        cols = pl.ds(pl.multiple_of((g % n_f) * tf, tf), tf)
        return (pltpu.make_async_copy(wg_hbm.at[layer, :, cols], wg_buf.at[slot], sem.at[0, slot]),
                pltpu.make_async_copy(wu_hbm.at[layer, :, cols], wu_buf.at[slot], sem.at[1, slot]),
                pltpu.make_async_copy(wd_hbm.at[layer, cols, :], wd_buf.at[slot], sem.at[2, slot]))

    @pl.when(first == 0)
    def _():
        for g in range(FFN_AHEAD):
            for copy in weight_copies(g):
                copy.start()

    x = x_ref[...]
    n_ref[...] = _rms(x, g_ref[...]).astype(_BF16)
    o_ref[...] = x

    def step(j, carry):
        g = first + j
        slot = g % FFN_RING

        @pl.when(g + FFN_AHEAD < total)
        def _():
            for copy in weight_copies(g + FFN_AHEAD):
                copy.start()

        for copy in weight_copies(g):
            copy.wait()
        n = n_ref[...]
        gate = jnp.dot(n, wg_buf[slot].astype(_BF16), preferred_element_type=_F32)
        up = jnp.dot(n, wu_buf[slot].astype(_BF16), preferred_element_type=_F32)
        h = (gate * jax.nn.sigmoid(gate)) * (up * 0.5)
        o_ref[...] += jnp.dot(h.astype(_BF16), wd_buf[slot].astype(_BF16),
                              preferred_element_type=_F32)
        return carry

    lax.fori_loop(0, n_f, step, 0)
    if final_norm:
        o_ref[...] = _rms(o_ref[...], gf_ref[...])


def _ffn(x, g, wg, wu, wd, layer, final_g=None, *, tm=1024, tf=256):
    m, d = x.shape
    f = wg.shape[2]
    assert m % tm == 0 and f % tf == 0
    final_norm = final_g is not None
    hbm = pl.BlockSpec(memory_space=pl.ANY)
    in_specs = [pl.BlockSpec((tm, d), lambda i: (i, 0)), pl.BlockSpec((1, d), lambda i: (0, 0)),
                hbm, hbm, hbm]
    args = [x, g.reshape(1, d), wg, wu, wd]
    if final_norm:
        in_specs.append(pl.BlockSpec((1, d), lambda i: (0, 0)))
        args.append(final_g.reshape(1, d))
    est = (2 * (2 * tm * d * 4) + tm * d * 2 + FFN_RING * 3 * d * tf * 4 + 3 * d * tf * 2
           + 4 * tm * tf * 4)
    return pl.pallas_call(
        functools.partial(_ffn_kernel, final_norm=final_norm, layer=layer, n_f=f // tf, tf=tf),
        grid=(m // tm,),
        in_specs=in_specs,
        out_specs=pl.BlockSpec((tm, d), lambda i: (i, 0)),
        out_shape=jax.ShapeDtypeStruct((m, d), _F32),
        scratch_shapes=[pltpu.VMEM((tm, d), _BF16),
                        pltpu.VMEM((FFN_RING, d, tf), _F32), pltpu.VMEM((FFN_RING, d, tf), _F32),
                        pltpu.VMEM((FFN_RING, tf, d), _F32),
                        pltpu.SemaphoreType.DMA((3, FFN_RING))],
        compiler_params=pltpu.CompilerParams(
            dimension_semantics=("arbitrary",),
            vmem_limit_bytes=_vmem_limit(est)),
        name="ffn_final" if final_norm else "ffn",
    )(*args)


def _snake(i, j, n):
    return jnp.where(i % 2 == 0, j, n - 1 - j)


def _store_heads(o_ref, col, y):
    for c in range(0, y.shape[1], HEAD_DIM):
        o_ref[(col + c) // HEAD_DIM] = y[:, c:c + HEAD_DIM].astype(o_ref.dtype)


def _proj_gate_kernel(x_ref, g_ref, w_ref, wf_ref, o_ref, gate_ref, n_ref):
    j = pl.program_id(1)

    @pl.when(j == 0)
    def _():
        n = _rms(x_ref[...], g_ref[...]).astype(_BF16)
        n_ref[...] = n
        gate_ref[...] = jnp.dot(n, wf_ref[...], preferred_element_type=_F32)

    for c in range(0, w_ref.shape[1], PROJ_CHUNK):
        y = jnp.dot(n_ref[...], w_ref[:, c:c + PROJ_CHUNK].astype(_BF16),
                    preferred_element_type=_F32)
        _store_heads(o_ref, c, y)


def _proj_rope_kernel(x_ref, g_ref, w_ref, cos_ref, sin_lo_ref, sin_hi_ref, o_ref, n_ref,
                      *, rope_blocks):
    @pl.when(pl.program_id(1) == 0)
    def _():
        n_ref[...] = _rms(x_ref[...], g_ref[...]).astype(_BF16)

    j = _snake(pl.program_id(0), pl.program_id(1), pl.num_programs(1))

    half = ROPE_DIMS // 2
    chunks = range(0, w_ref.shape[1], PROJ_CHUNK)

    def project(col):
        return jnp.dot(n_ref[...], w_ref[:, col:col + PROJ_CHUNK].astype(_BF16),
                       preferred_element_type=_F32)

    @pl.when(j < rope_blocks)
    def _():
        reps = PROJ_CHUNK // HEAD_DIM
        cos, sin_lo, sin_hi = (jnp.tile(t[...], (1, reps))
                               for t in (cos_ref, sin_lo_ref, sin_hi_ref))
        for col in chunks:
            y = project(col)
            _store_heads(o_ref, col, y * cos + pltpu.roll(y, PROJ_CHUNK - half, 1) * sin_lo
                         + pltpu.roll(y, half, 1) * sin_hi)

    @pl.when(j >= rope_blocks)
    def _():
        for col in chunks:
            _store_heads(o_ref, col, project(col))


def _rope_tables(seq):
    half = ROPE_DIMS // 2
    freqs = ROPE_THETA ** (-jnp.arange(half, dtype=_F32) / half)
    ang = jnp.arange(seq, dtype=_F32)[:, None] * freqs[None, :]
    cos, sin = jnp.cos(ang), jnp.sin(ang)
    pad = jnp.zeros((seq, HEAD_DIM - ROPE_DIMS), _F32)
    zero = jnp.zeros((seq, half), _F32)
    cos_t = jnp.concatenate([cos, cos, pad + 1.0], axis=1)
    sin_lo = jnp.concatenate([-sin, zero, pad], axis=1)
    sin_hi = jnp.concatenate([zero, sin, pad], axis=1)
    return cos_t, sin_lo, sin_hi


def _proj(x, g, w, layer, n_out, *, w_gate=None, rope_seq=None, rope_cols=0, tm=1024,
          tn=1024):
    m, d = x.shape
    assert m % tm == 0 and n_out % tn == 0 and n_out <= w.shape[2] and tn % PROJ_CHUNK == 0
    n_j = n_out // tn
    in_specs = [
        pl.BlockSpec((tm, d), lambda i, j: (i, 0)),
        pl.BlockSpec((1, d), lambda i, j: (0, 0)),
        pl.BlockSpec((d, tn), lambda i, j: (layer, _snake(i, j, n_j))),
    ]
    args = [x, g.reshape(1, d), w.reshape(-1, w.shape[2])]
    out_specs = pl.BlockSpec((tn // HEAD_DIM, tm, HEAD_DIM),
                             lambda i, j: (_snake(i, j, n_j), i, 0))
    out_shape = jax.ShapeDtypeStruct((n_out // HEAD_DIM, m, HEAD_DIM), _BF16)
    est = (2 * tm * d * 4 + tm * d * 2 + 2 * d * tn * 4 + d * tn * 2 + 2 * tm * tn * 2
           + 2 * tm * tn * 4)
    if w_gate is not None:
        ng = w_gate.shape[1]
        in_specs.append(pl.BlockSpec((d, ng), lambda i, j: (0, 0)))
        args.append(w_gate)
        out_specs = (out_specs, pl.BlockSpec((tm, ng), lambda i, j: (i, 0)))
        out_shape = (out_shape, jax.ShapeDtypeStruct((m, ng), _F32))
        body = _proj_gate_kernel
        name = "proj_gate"
    else:
        assert rope_seq % tm == 0 and rope_cols % tn == 0
        seq_blocks = rope_seq // tm
        tab_spec = pl.BlockSpec((tm, HEAD_DIM), lambda i, j: (i % seq_blocks, 0))
        in_specs += [tab_spec] * 3
        args += list(_rope_tables(rope_seq))
        est += 3 * 2 * tm * HEAD_DIM * 4
        body = functools.partial(_proj_rope_kernel, rope_blocks=rope_cols // tn)
        name = "proj_rope"
    return pl.pallas_call(
        body,
        grid=(m // tm, n_out // tn),
        in_specs=in_specs,
        out_specs=out_specs,
        out_shape=out_shape,
        scratch_shapes=[pltpu.VMEM((tm, d), _BF16)],
        compiler_params=pltpu.CompilerParams(
            dimension_semantics=("parallel", "arbitrary"),
            vmem_limit_bytes=_vmem_limit(est)),
        name=name,
    )(*args)


def _split3(x):
    hi = x.astype(_BF16)
    r = x - hi.astype(_F32)
    mid = r.astype(_BF16)
    lo = (r - mid.astype(_F32)).astype(_BF16)
    return hi, mid, lo


def _gates_kernel(gp_ref, b_ref, col_ref, row_ref, *, blk):
    seq = gp_ref.shape[1]
    z = gp_ref[0] + b_ref[...]
    log_f = jnp.minimum(z, 0.0) - _log1p_exp_neg_abs(z)
    r = lax.broadcasted_iota(jnp.int32, (blk, blk), 0)
    c = lax.broadcasted_iota(jnp.int32, (blk, blk), 1)
    tri = jnp.where(c <= r, 1.0, 0.0).astype(_BF16)
    carry = jnp.zeros((1, log_f.shape[1]), _F32)
    for s in range(seq // blk):
        part = log_f[s * blk:(s + 1) * blk]
        cs = carry
        for p in _split3(part):
            cs = cs + jnp.dot(tri, p, preferred_element_type=_F32)
        col_ref[0, s * blk:(s + 1) * blk, :] = cs
        carry = cs[blk - 1:blk, :]
    row_ref[0] = jnp.transpose(col_ref[0])[:row_ref.shape[1], :]


def _gates(gate_pre, b_forget, batch, seq):
    ng = gate_pre.shape[1]
    nh = b_forget.shape[0]
    b_pad = jnp.zeros((1, ng), _F32).at[0, :nh].set(b_forget)
    return pl.pallas_call(
        functools.partial(_gates_kernel, blk=ATT_BLOCK),
        grid=(batch,),
        in_specs=[pl.BlockSpec((1, seq, ng), lambda b: (b, 0, 0)),
                  pl.BlockSpec((1, ng), lambda b: (0, 0))],
        out_specs=(pl.BlockSpec((1, seq, ng), lambda b: (b, 0, 0)),
                   pl.BlockSpec((1, nh, seq), lambda b: (b, 0, 0))),
        out_shape=(jax.ShapeDtypeStruct((batch, seq, ng), _F32),
                   jax.ShapeDtypeStruct((batch, nh, seq), _F32)),
        compiler_params=pltpu.CompilerParams(dimension_semantics=("parallel",)),
        name="gates",
    )(gate_pre.reshape(batch, seq, ng), b_pad)


def _transpose_bf16(x):
    return jnp.transpose(x)


def _block_iotas(blk):
    key = lax.broadcasted_iota(jnp.int32, (blk, blk), 0)
    qry = lax.broadcasted_iota(jnp.int32, (blk, blk), 1)
    return key, qry


def _rows(ref, j, blk):
    return ref[pl.ds(pl.multiple_of(j * blk, blk), blk), :]


def _cols(ref, j, blk):
    return ref[:, pl.ds(pl.multiple_of(j * blk, blk), blk)]


def _chain_sweep(logits, finish, states, chains, base, n_shared):
    def wave(w):
        return [(c, base + c - w, w, w == 0) for c in range(w, chains)]

    def shared(step):
        j = jnp.maximum(base - 1 - step, 0)
        return [(c, j, c + 1 + step, False) for c in range(chains)]

    states = list(states)
    us = logits(wave(0))
    for w in range(chains):
        if w + 1 < chains:
            ahead = logits(wave(w + 1))
        elif n_shared is not None:
            ahead = logits(shared(0))
        new = finish(wave(w), us, states[w:])
        states[w:] = new
        us = ahead
    if n_shared is None:
        return states

    def body(step, carry):
        us, sts = carry
        ahead = logits(shared(step + 1))
        return tuple(ahead), tuple(finish(shared(step), list(us), list(sts)))

    return lax.fori_loop(0, n_shared, body, (tuple(us), tuple(states)))[1]


def _sb_head(qt, k_ref, vt_ref, o_ref, tile, blk, chains, single_tile):
    scale2 = HEAD_DIM ** -0.5 * LOG2_E
    key, qry = _block_iotas(blk)
    strict = key < qry
    suffix = jnp.where(qry > key, 1.0, 0.0).astype(_BF16)

    def gates(z2, diagonal):
        log_beta = jnp.minimum(z2, 0.0) - jnp.log2(1.0 + jnp.exp2(-jnp.abs(z2)))
        log_not_beta = log_beta - z2
        if diagonal:
            log_not_beta = jnp.where(strict, log_not_beta, 0.0)
        hi = log_not_beta.astype(_BF16)
        lo = (log_not_beta - hi.astype(_F32)).astype(_BF16)
        return log_beta, log_not_beta, hi, lo

    def logits(items):
        return [jnp.dot(_rows(k_ref, j, blk), qt[c], preferred_element_type=_F32) * scale2
                for c, j, _, _ in items]

    def finish(items, z2, states):
        g = [gates(z, diag) for z, (_, _, _, diag) in zip(z2, items)]
        after = [jnp.dot(suffix, hi, preferred_element_type=_F32)
                 + jnp.dot(suffix, lo, preferred_element_type=_F32) + carry
                 for (_, _, hi, lo), (carry, _) in zip(g, states)]
        new = []
        for (log_beta, log_not_beta, _, _), aft, (carry, acc), (_, j, _, diag) in zip(
                g, after, states, items):
            w = jnp.exp2(log_beta + aft)
            if diag:
                w = jnp.where(strict, w, 0.0)
            acc = acc + jnp.dot(_cols(vt_ref, j, blk), w.astype(_BF16),
                                preferred_element_type=_F32)
            new.append((carry + jnp.sum(log_not_beta, axis=0, keepdims=True), acc))
        return new

    init = (jnp.zeros((1, blk), _F32), jnp.zeros((HEAD_DIM, blk), _F32))
    base = tile * chains
    states = _chain_sweep(logits, finish, [init] * chains, chains, base,
                          None if single_tile else base)
    for c in range(chains):
        o_ref[c * blk:(c + 1) * blk, :] = jnp.transpose(states[c][1].astype(o_ref.dtype))


def _softmax_finish(values, shifts=None):
    def finish(items, us, states):
        mids = []
        for u, (c, _, _, _), (m, l, _) in zip(us, items, states):
            u_max = jnp.max(u, axis=0, keepdims=True)
            shift = None if shifts is None else shifts[c]
            if shift is not None:
                u_max = u_max + shift
            m_new = jnp.maximum(m, u_max)
            alpha = jnp.exp2(m - m_new)
            p = jnp.exp2(u - m_new if shift is None else u + (shift - m_new))
            mids.append((m_new, alpha, alpha * l + jnp.sum(p, axis=0, keepdims=True),
                         p.astype(_BF16)))
        return [(m_new, l_new, alpha * acc + jnp.dot(values(j), p, preferred_element_type=_F32))
                for (m_new, alpha, l_new, p), (_, j, _, _), (_, _, acc) in zip(mids, items, states)]
    return finish


def _softmax_init(blk):
    return (jnp.full((1, blk), NEG_INF, _F32), jnp.zeros((1, blk), _F32),
            jnp.zeros((HEAD_DIM, blk), _F32))


def _softmax_store(o_ref, states, blk):
    for c, (_, l, acc) in enumerate(states):
        o_ref[c * blk:(c + 1) * blk, :] = jnp.transpose((acc / l).astype(o_ref.dtype))


FOX_BIAS_PARTS = 3


def _fox_augment_keys(k_ref, cfc_ref, hh, kaug_ref):
    seq = k_ref.shape[0]
    lane = lax.broadcasted_iota(jnp.int32, (seq, cfc_ref.shape[2]), 1)
    cf = jnp.sum(jnp.where(lane == hh, cfc_ref[0], 0.0), axis=1, keepdims=True)
    parts = _split3(cf * -(HEAD_DIM ** 0.5))
    lane = lax.broadcasted_iota(jnp.int32, (seq, HEAD_DIM), 1)
    extra = jnp.zeros((seq, HEAD_DIM), _F32)
    for i, part in enumerate(parts):
        extra = jnp.where(lane == i, part.astype(_F32), extra)
    kaug_ref[:, :HEAD_DIM] = k_ref[...]
    kaug_ref[:, HEAD_DIM:] = extra.astype(_BF16)


def _fox_head(qt, kaug_ref, vt_ref, cfr_ref, o_ref, hh, tile, blk, chains, single_tile):
    scale2 = HEAD_DIM ** -0.5 * LOG2_E
    key, qry = _block_iotas(blk)
    causal = key <= qry
    sub = lax.broadcasted_iota(jnp.int32, (HEAD_DIM, blk), 0)
    ones_rows = jnp.where(sub < FOX_BIAS_PARTS, 1.0, 0.0).astype(_BF16)
    base = tile * chains
    qt_aug = [jnp.concatenate([qt[c], ones_rows], axis=0) for c in range(chains)]
    cf_q = [cfr_ref[0, hh, pl.ds(base + c, 1), :] * LOG2_E for c in range(chains)]

    def logit(c, j, dist, diagonal):
        u = jnp.dot(_rows(kaug_ref, j, blk), qt_aug[c], preferred_element_type=_F32) * scale2
        return jnp.where(causal, u, NEG_INF) if diagonal else u

    states = _chain_sweep(lambda items: [logit(*it) for it in items],
                          _softmax_finish(lambda j: _cols(vt_ref, j, blk), cf_q),
                          [_softmax_init(blk)] * chains, chains, base,
                          None if single_tile else base)
    _softmax_store(o_ref, states, blk)


def _head_refs(gi, q_ref, k_ref, v_ref, o_ref):
    cols = pl.ds(pl.multiple_of(gi * HEAD_DIM, HEAD_DIM), HEAD_DIM)
    return q_ref.at[gi], k_ref.at[gi], v_ref.at[gi], o_ref.at[:, cols]


def _query_chains(q, blk, chains):
    qt_all = _transpose_bf16(q[...])
    return [qt_all[:, c * blk:(c + 1) * blk] for c in range(chains)]


def _attn_even_kernel(q_ref, k_ref, v_ref, cfc_ref, cfr_ref, o_ref, vt_ref, kaug_ref,
                      *, blk, chains, single_tile, group):
    first_head = pl.program_id(1) * group
    tile = pl.program_id(2)

    def head(gi, stick_breaking):
        q, k, v, o = _head_refs(gi, q_ref, k_ref, v_ref, o_ref)
        vt_ref[...] = _transpose_bf16(v[...])
        qt = _query_chains(q, blk, chains)
        if stick_breaking:
            _sb_head(qt, k, vt_ref, o, tile, blk, chains, single_tile)
        else:
            hh = first_head + gi - N_SB_HEADS
            _fox_augment_keys(k, cfc_ref, hh, kaug_ref)
            _fox_head(qt, kaug_ref, vt_ref, cfr_ref, o, hh, tile, blk, chains, single_tile)

    def heads(stick_breaking):
        def body(gi, carry):
            head(gi, stick_breaking)
            return carry
        lax.fori_loop(0, group, body, 0)

    pl.when(first_head < N_SB_HEADS)(functools.partial(heads, True))
    pl.when(first_head >= N_SB_HEADS)(functools.partial(heads, False))


def _attn_specs(batch, seq, n_heads, rows, group):
    assert n_heads % group == 0
    nq = seq // rows
    ng = n_heads // group
    q_spec = pl.BlockSpec((group, rows, HEAD_DIM), lambda b, h, i: (h, b * nq + i, 0))
    k_spec = pl.BlockSpec((group, seq, HEAD_DIM), lambda b, h, i: (ng + h, b, 0))
    v_spec = pl.BlockSpec((group, seq, HEAD_DIM), lambda b, h, i: (2 * ng + h, b, 0))
    o_spec = pl.BlockSpec((rows, group * HEAD_DIM), lambda b, h, i: (b * nq + i, h))
    return nq, ng, q_spec, k_spec, v_spec, o_spec


def _attn_even(qkv, cf_col, cf_row, batch, seq, *, blk=ATT_BLOCK, chains=ATT_CHAINS,
               group=ATT_HEAD_GROUP):
    n_heads = N_SB_HEADS + N_FOX_HEADS
    assert N_SB_HEADS % group == 0
    nq, ng, q_spec, k_spec, v_spec, o_spec = _attn_specs(batch, seq, n_heads, blk * chains, group)
    gate_lanes = cf_col.shape[2]
    nk = seq // blk
    cf_row = cf_row.reshape(batch, N_FOX_HEADS, nk, blk)
    return pl.pallas_call(
        functools.partial(_attn_even_kernel, blk=blk, chains=chains, single_tile=nq == 1,
                          group=group),
        grid=(batch, ng, nq),
        in_specs=[q_spec, k_spec, v_spec,
                  pl.BlockSpec((1, seq, gate_lanes), lambda b, h, i: (b, 0, 0)),
                  pl.BlockSpec((1, N_FOX_HEADS, nk, blk), lambda b, h, i: (b, 0, 0, 0))],
        out_specs=o_spec,
        out_shape=jax.ShapeDtypeStruct((batch * seq, n_heads * HEAD_DIM), _BF16),
        scratch_shapes=[pltpu.VMEM((HEAD_DIM, seq), _BF16),
                        pltpu.VMEM((seq, 2 * HEAD_DIM), _BF16)],
        compiler_params=pltpu.CompilerParams(
            dimension_semantics=("parallel", "parallel", "arbitrary")),
        name="attn_even",
    )(qkv, qkv, qkv, cf_col, cf_row)


def _dilated_bias_tables(blk):
    n_near = (DIL_W2 + blk - 1) // blk + 1
    assert blk % DIL_D3 == 0 and n_near * blk - (blk - 1) > DIL_W2
    d = jnp.arange(n_near + 1, dtype=jnp.int32)[:, None, None]
    s = jnp.arange(blk, dtype=jnp.int32)[None, :, None]
    t = jnp.arange(blk, dtype=jnp.int32)[None, None, :]
    delta = d * blk + t - s
    count = ((delta <= DIL_W1).astype(_F32)
             + ((delta % DIL_D2 == 0) & (delta <= DIL_W2)).astype(_F32)
             + (delta % DIL_D3 == 0).astype(_F32))
    count = jnp.where(delta >= 0, count, 0.0)
    return jnp.where(count > 0, jnp.log2(jnp.maximum(count, 1.0)), NEG_INF), n_near


def _attn_dilated_kernel(q_ref, k_ref, v_ref, bias_ref, o_ref, vt_ref, *, blk, chains,
                         n_near, single_tile, group):
    tile = pl.program_id(2)
    scale2 = HEAD_DIM ** -0.5 * LOG2_E
    base = tile * chains

    def head(gi, carry):
        q, k, v, o = _head_refs(gi, q_ref, k_ref, v_ref, o_ref)
        vt_ref[...] = _transpose_bf16(v[...])
        qt = _query_chains(q, blk, chains)

        def logit(c, j, dist, diagonal):
            table = min(dist, n_near) if isinstance(dist, int) else jnp.minimum(dist, n_near)
            return (jnp.dot(_rows(k, j, blk), qt[c], preferred_element_type=_F32) * scale2
                    + bias_ref[table])

        states = _chain_sweep(lambda items: [logit(*it) for it in items],
                              _softmax_finish(lambda j: _cols(vt_ref, j, blk)),
                              [_softmax_init(blk)] * chains, chains, base,
                              None if single_tile else base)
        _softmax_store(o, states, blk)
        return carry

    lax.fori_loop(0, group, head, 0)


def _attn_dilated(qkv, batch, seq, n_heads, *, blk=ATT_BLOCK, chains=ATT_CHAINS,
                  group=ATT_HEAD_GROUP):
    assert seq <= DIL_D3 * DIL_W1
    nq, ng, q_spec, k_spec, v_spec, o_spec = _attn_specs(batch, seq, n_heads, blk * chains, group)
    bias, n_near = _dilated_bias_tables(blk)
    return pl.pallas_call(
        functools.partial(_attn_dilated_kernel, blk=blk, chains=chains, n_near=n_near,
                          single_tile=nq == 1, group=group),
        grid=(batch, ng, nq),
        in_specs=[q_spec, k_spec, v_spec,
                  pl.BlockSpec(bias.shape, lambda b, h, i: (0, 0, 0))],
        out_specs=o_spec,
        out_shape=jax.ShapeDtypeStruct((batch * seq, n_heads * HEAD_DIM), _BF16),
        scratch_shapes=[pltpu.VMEM((HEAD_DIM, seq), _BF16)],
        compiler_params=pltpu.CompilerParams(
            dimension_semantics=("parallel", "parallel", "arbitrary")),
        name="attn_dilated",
    )(qkv, qkv, qkv, bias)


def _outproj_kernel(o_ref, w_ref, x_ref, y_ref):
    y_ref[...] = x_ref[...] + jnp.dot(o_ref[...], w_ref[...].astype(_BF16),
                                      preferred_element_type=_F32)


def _outproj(o, w, layer, x, *, tm=1024, tn=1024):
    m, d_in = o.shape
    d = w.shape[2]
    n_j = d // tn
    est = 2 * tm * d_in * 2 + 2 * d_in * tn * 4 + d_in * tn * 2 + 4 * tm * tn * 4
    return pl.pallas_call(
        _outproj_kernel,
        grid=(m // tm, d // tn),
        in_specs=[pl.BlockSpec((tm, d_in), lambda i, j: (i, 0)),
                  pl.BlockSpec((None, d_in, tn), lambda i, j: (layer, 0, _snake(i, j, n_j))),
                  pl.BlockSpec((tm, tn), lambda i, j: (i, _snake(i, j, n_j)))],
        out_specs=pl.BlockSpec((tm, tn), lambda i, j: (i, _snake(i, j, n_j))),
        out_shape=jax.ShapeDtypeStruct((m, d), _F32),
        compiler_params=pltpu.CompilerParams(
            dimension_semantics=("parallel", "parallel"),
            vmem_limit_bytes=_vmem_limit(est)),
        name="outproj",
    )(o, w, x)


def kernel(x, norm_g, ffn1_w_gate, ffn1_w_up, ffn1_w_down, ffn2_w_gate, ffn2_w_up,
           ffn2_w_down, even_w_in, even_b_forget, even_w_out, odd_w_qkv, odd_w_out,
           final_norm_g):
    batch, seq, d = x.shape
    depth = norm_g.shape[0]
    n_heads = d // HEAD_DIM
    assert n_heads == N_SB_HEADS + N_FOX_HEADS
    xf = x.reshape(batch * seq, d)
    for layer in range(depth):
        j = layer // 2
        xf = _ffn(xf, norm_g[layer, 0], ffn1_w_gate, ffn1_w_up, ffn1_w_down, layer)
        if layer % 2 == 0:
            w_gate = jnp.zeros((d, V7X_LANES), _BF16).at[:, :N_FOX_HEADS].set(
                even_w_in[j, :, 3 * d:].astype(_BF16))
            qkv, gate_pre = _proj(xf, norm_g[layer, 1], even_w_in, j, 3 * d, w_gate=w_gate)
            cf_col, cf_row = _gates(gate_pre, even_b_forget[j], batch, seq)
            o = _attn_even(qkv, cf_col, cf_row, batch, seq)
            xf = _outproj(o, even_w_out, j, xf)
        else:
            qkv = _proj(xf, norm_g[layer, 1], odd_w_qkv, j, 3 * d, rope_seq=seq,
                        rope_cols=2 * d)
            o = _attn_dilated(qkv, batch, seq, n_heads)
            xf = _outproj(o, odd_w_out, j, xf)
        last = layer == depth - 1
        xf = _ffn(xf, norm_g[layer, 2], ffn2_w_gate, ffn2_w_up, ffn2_w_down, layer,
                  final_norm_g if last else None)
    return xf.reshape(batch, seq, d)
```

```python
import functools

import jax
import jax.numpy as jnp
from jax import lax
from jax.experimental import pallas as pl
from jax.experimental.pallas import tpu as pltpu

_F32 = jnp.float32
_BF16 = jnp.bfloat16

HEAD_DIM = 128
N_SB_HEADS = 8
N_FOX_HEADS = 8
ROPE_DIMS = HEAD_DIM // 4
ROPE_THETA = 500000.0
RMS_EPS = 1e-6
NEG_INF = -1e30
LOG2_E = 1.4426950408889634
DIL_W1, DIL_W2, DIL_D2, DIL_D3 = 128, 512, 4, 16

V7X_LANES = 128
V7X_VMEM_BYTES = 64 * 1024 * 1024
V7X_SCOPED_VMEM_DEFAULT = 32 * 1024 * 1024
VMEM_CAP_BYTES = V7X_VMEM_BYTES * 15 // 16

FFN_UNROLL = 2
FFN_RING = 2 * FFN_UNROLL
PROJ_CHUNK = 256
ATT_BLOCK = 256
ATT_CHAINS = 8
ATT_HEAD_GROUP = 4


def _vmem_limit(block_bytes):
    return int(min(VMEM_CAP_BYTES, max(V7X_SCOPED_VMEM_DEFAULT, block_bytes * 5 // 4)))


def _rms(x, g):
    ms = jnp.mean(x * x, axis=-1, keepdims=True)
    return x * lax.rsqrt(ms + RMS_EPS) * g


def _log1p_exp_neg_abs(z):
    return jnp.log(1.0 + jnp.exp(-jnp.abs(z)))


def _ffn_kernel(*refs, final_norm, layer, n_f, tf):
    if final_norm:
        (x_hbm, g_ref, wg_hbm, wu_hbm, wd_hbm, gf_ref, o_ref,
         n_ref, x_buf, wg_buf, wu_buf, wd_buf, w_sem, x_sem) = refs
    else:
        (x_hbm, g_ref, wg_hbm, wu_hbm, wd_hbm, o_ref,
         n_ref, x_buf, wg_buf, wu_buf, wd_buf, w_sem, x_sem) = refs
    tm = x_buf.shape[0]
    tile = pl.program_id(0)
    last_tile = pl.num_programs(0) - 1

    first = tile * n_f
    total = pl.num_programs(0) * n_f

    def weight_copies(g):
        slot = g % FFN_RING
        cols = pl.ds(pl.multiple_of((g % n_f) * tf, tf), tf)
        return (pltpu.make_async_copy(wg_hbm.at[layer, :, cols], wg_buf.at[slot], w_sem.at[0, slot]),
                pltpu.make_async_copy(wu_hbm.at[layer, :, cols], wu_buf.at[slot], w_sem.at[1, slot]),
                pltpu.make_async_copy(wd_hbm.at[layer, cols, :], wd_buf.at[slot], w_sem.at[2, slot]))

    def x_copy(i):
        rows = pl.ds(pl.multiple_of(i * tm, tm), tm)
        return pltpu.make_async_copy(x_hbm.at[rows, :], x_buf, x_sem.at[0])

    @pl.when(tile == 0)
    def _():
        x_copy(tile).start()
        for g in range(FFN_UNROLL):
            for copy in weight_copies(g):
                copy.start()

    x_copy(tile).wait()
    x = x_buf[...]
    n_ref[...] = _rms(x, g_ref[...]).astype(_BF16)
    o_ref[...] = x

    @pl.when(tile < last_tile)
    def _():
        x_copy(tile + 1).start()

    def block(g):
        slot = g % FFN_RING
        n = n_ref[...]
        gate = jnp.dot(n, wg_buf[slot].astype(_BF16), preferred_element_type=_F32)
        up = jnp.dot(n, wu_buf[slot].astype(_BF16), preferred_element_type=_F32)
        h = (gate * jax.nn.sigmoid(gate)) * (up * 0.5)
        o_ref[...] += jnp.dot(h.astype(_BF16), wd_buf[slot].astype(_BF16),
                              preferred_element_type=_F32)

    def trip(t, carry):
        g0 = first + t * FFN_UNROLL
        for u in range(FFN_UNROLL):
            @pl.when(g0 + FFN_UNROLL + u < total)
            def _():
                for copy in weight_copies(g0 + FFN_UNROLL + u):
                    copy.start()
        for u in range(FFN_UNROLL):
            for copy in weight_copies(g0 + u):
                copy.wait()
        for u in range(FFN_UNROLL):
            block(g0 + u)
        return carry

    lax.fori_loop(0, n_f // FFN_UNROLL, trip, 0)
    if final_norm:
        o_ref[...] = _rms(o_ref[...], gf_ref[...])


def _ffn(x, g, wg, wu, wd, layer, final_g=None, *, tm=1024, tf=256):
    m, d = x.shape
    f = wg.shape[2]
    n_f = f // tf
    assert m % tm == 0 and f % tf == 0 and n_f % FFN_UNROLL == 0
    final_norm = final_g is not None
    hbm = pl.BlockSpec(memory_space=pl.ANY)
    in_specs = [hbm, pl.BlockSpec((1, d), lambda i: (0, 0)), hbm, hbm, hbm]
    args = [x, g.reshape(1, d), wg, wu, wd]
    if final_norm:
        in_specs.append(pl.BlockSpec((1, d), lambda i: (0, 0)))
        args.append(final_g.reshape(1, d))
    est = (3 * tm * d * 4 + tm * d * 2 + FFN_RING * 3 * d * tf * 4 + 3 * d * tf * 2
           + 4 * tm * tf * 4)
    return pl.pallas_call(
        functools.partial(_ffn_kernel, final_norm=final_norm, layer=layer, n_f=n_f, tf=tf),
        grid=(m // tm,),
        in_specs=in_specs,
        out_specs=pl.BlockSpec((tm, d), lambda i: (i, 0)),
        out_shape=jax.ShapeDtypeStruct((m, d), _F32),
        scratch_shapes=[pltpu.VMEM((tm, d), _BF16), pltpu.VMEM((tm, d), _F32),
                        pltpu.VMEM((FFN_RING, d, tf), _F32), pltpu.VMEM((FFN_RING, d, tf), _F32),
                        pltpu.VMEM((FFN_RING, tf, d), _F32),
                        pltpu.SemaphoreType.DMA((3, FFN_RING)), pltpu.SemaphoreType.DMA((1,))],
        compiler_params=pltpu.CompilerParams(
            dimension_semantics=("arbitrary",),
            vmem_limit_bytes=_vmem_limit(est)),
        name="ffn_final" if final_norm else "ffn",
    )(*args)


def _snake(i, j, n):
    return jnp.where(i % 2 == 0, j, n - 1 - j)


def _store_heads(o_ref, col, y):
    for c in range(0, y.shape[1], HEAD_DIM):
        o_ref[(col + c) // HEAD_DIM] = y[:, c:c + HEAD_DIM].astype(o_ref.dtype)


def _proj_gate_kernel(x_ref, g_ref, w_ref, wf_ref, o_ref, gate_ref, n_ref):
    j = pl.program_id(1)

    @pl.when(j == 0)
    def _():
        n = _rms(x_ref[...], g_ref[...]).astype(_BF16)
        n_ref[...] = n
        gate_ref[...] = jnp.dot(n, wf_ref[...], preferred_element_type=_F32)

    for c in range(0, w_ref.shape[1], PROJ_CHUNK):
        y = jnp.dot(n_ref[...], w_ref[:, c:c + PROJ_CHUNK].astype(_BF16),
                    preferred_element_type=_F32)
        _store_heads(o_ref, c, y)


def _proj_rope_kernel(x_ref, g_ref, w_ref, cos_ref, sin_lo_ref, sin_hi_ref, o_ref, n_ref,
                      *, rope_blocks):
    @pl.when(pl.program_id(1) == 0)
    def _():
        n_ref[...] = _rms(x_ref[...], g_ref[...]).astype(_BF16)

    j = _snake(pl.program_id(0), pl.program_id(1), pl.num_programs(1))

    half = ROPE_DIMS // 2
    chunks = range(0, w_ref.shape[1], PROJ_CHUNK)

    def project(col):
        return jnp.dot(n_ref[...], w_ref[:, col:col + PROJ_CHUNK].astype(_BF16),
                       preferred_element_type=_F32)

    @pl.when(j < rope_blocks)
    def _():
        reps = PROJ_CHUNK // HEAD_DIM
        cos, sin_lo, sin_hi = (jnp.tile(t[...], (1, reps))
                               for t in (cos_ref, sin_lo_ref, sin_hi_ref))
        for col in chunks:
            y = project(col)
            _store_heads(o_ref, col, y * cos + pltpu.roll(y, PROJ_CHUNK - half, 1) * sin_lo
                         + pltpu.roll(y, half, 1) * sin_hi)

    @pl.when(j >= rope_blocks)
    def _():
        for col in chunks:
            _store_heads(o_ref, col, project(col))


def _rope_tables(seq):
    half = ROPE_DIMS // 2
    freqs = ROPE_THETA ** (-jnp.arange(half, dtype=_F32) / half)
    ang = jnp.arange(seq, dtype=_F32)[:, None] * freqs[None, :]
    cos, sin = jnp.cos(ang), jnp.sin(ang)
    pad = jnp.zeros((seq, HEAD_DIM - ROPE_DIMS), _F32)
    zero = jnp.zeros((seq, half), _F32)
    cos_t = jnp.concatenate([cos, cos, pad + 1.0], axis=1)
    sin_lo = jnp.concatenate([-sin, zero, pad], axis=1)
    sin_hi = jnp.concatenate([zero, sin, pad], axis=1)
    return cos_t, sin_lo, sin_hi


def _proj(x, g, w, layer, n_out, *, w_gate=None, rope_seq=None, rope_cols=0, tm=1024,
          tn=1024):
    m, d = x.shape
    assert m % tm == 0 and n_out % tn == 0 and n_out <= w.shape[2] and tn % PROJ_CHUNK == 0
    n_j = n_out // tn
    in_specs = [
        pl.BlockSpec((tm, d), lambda i, j: (i, 0)),
        pl.BlockSpec((1, d), lambda i, j: (0, 0)),
        pl.BlockSpec((d, tn), lambda i, j: (layer, _snake(i, j, n_j))),
    ]
    args = [x, g.reshape(1, d), w.reshape(-1, w.shape[2])]
    out_specs = pl.BlockSpec((tn // HEAD_DIM, tm, HEAD_DIM),
                             lambda i, j: (_snake(i, j, n_j), i, 0))
    out_shape = jax.ShapeDtypeStruct((n_out // HEAD_DIM, m, HEAD_DIM), _BF16)
    est = (2 * tm * d * 4 + tm * d * 2 + 2 * d * tn * 4 + d * tn * 2 + 2 * tm * tn * 2
           + 2 * tm * tn * 4)
    if w_gate is not None:
        ng = w_gate.shape[1]
        in_specs.append(pl.BlockSpec((d, ng), lambda i, j: (0, 0)))
        args.append(w_gate)
        out_specs = (out_specs, pl.BlockSpec((tm, ng), lambda i, j: (i, 0)))
        out_shape = (out_shape, jax.ShapeDtypeStruct((m, ng), _F32))
        body = _proj_gate_kernel
        name = "proj_gate"
    else:
        assert rope_seq % tm == 0 and rope_cols % tn == 0
        seq_blocks = rope_seq // tm
        tab_spec = pl.BlockSpec((tm, HEAD_DIM), lambda i, j: (i % seq_blocks, 0))
        in_specs += [tab_spec] * 3
        args += list(_rope_tables(rope_seq))
        est += 3 * 2 * tm * HEAD_DIM * 4
        body = functools.partial(_proj_rope_kernel, rope_blocks=rope_cols // tn)
        name = "proj_rope"
    return pl.pallas_call(
        body,
        grid=(m // tm, n_out // tn),
        in_specs=in_specs,
        out_specs=out_specs,
        out_shape=out_shape,
        scratch_shapes=[pltpu.VMEM((tm, d), _BF16)],
        compiler_params=pltpu.CompilerParams(
            dimension_semantics=("parallel", "arbitrary"),
            vmem_limit_bytes=_vmem_limit(est)),
        name=name,
    )(*args)


def _split3(x):
    hi = x.astype(_BF16)
    r = x - hi.astype(_F32)
    mid = r.astype(_BF16)
    lo = (r - mid.astype(_F32)).astype(_BF16)
    return hi, mid, lo


def _gates_kernel(gp_ref, b_ref, col_ref, row_ref, *, blk):
    seq = gp_ref.shape[1]
    z = gp_ref[0] + b_ref[...]
    log_f = jnp.minimum(z, 0.0) - _log1p_exp_neg_abs(z)
    r = lax.broadcasted_iota(jnp.int32, (blk, blk), 0)
    c = lax.broadcasted_iota(jnp.int32, (blk, blk), 1)
    tri = jnp.where(c <= r, 1.0, 0.0).astype(_BF16)
    carry = jnp.zeros((1, log_f.shape[1]), _F32)
    for s in range(seq // blk):
        part = log_f[s * blk:(s + 1) * blk]
        cs = carry
        for p in _split3(part):
            cs = cs + jnp.dot(tri, p, preferred_element_type=_F32)
        col_ref[0, s * blk:(s + 1) * blk, :] = cs
        carry = cs[blk - 1:blk, :]
    row_ref[0] = jnp.transpose(col_ref[0])[:row_ref.shape[1], :]


def _gates(gate_pre, b_forget, batch, seq):
    ng = gate_pre.shape[1]
    nh = b_forget.shape[0]
    b_pad = jnp.zeros((1, ng), _F32).at[0, :nh].set(b_forget)
    return pl.pallas_call(
        functools.partial(_gates_kernel, blk=ATT_BLOCK),
        grid=(batch,),
        in_specs=[pl.BlockSpec((1, seq, ng), lambda b: (b, 0, 0)),
                  pl.BlockSpec((1, ng), lambda b: (0, 0))],
        out_specs=(pl.BlockSpec((1, seq, ng), lambda b: (b, 0, 0)),
                   pl.BlockSpec((1, nh, seq), lambda b: (b, 0, 0))),
        out_shape=(jax.ShapeDtypeStruct((batch, seq, ng), _F32),
                   jax.ShapeDtypeStruct((batch, nh, seq), _F32)),
        compiler_params=pltpu.CompilerParams(dimension_semantics=("parallel",)),
        name="gates",
    )(gate_pre.reshape(batch, seq, ng), b_pad)


def _transpose_bf16(x):
    return jnp.transpose(x)


def _block_iotas(blk):
    key = lax.broadcasted_iota(jnp.int32, (blk, blk), 0)
    qry = lax.broadcasted_iota(jnp.int32, (blk, blk), 1)
    return key, qry


def _rows(ref, j, blk):
    return ref[pl.ds(pl.multiple_of(j * blk, blk), blk), :]


def _cols(ref, j, blk):
    return ref[:, pl.ds(pl.multiple_of(j * blk, blk), blk)]


def _chain_sweep(logits, finish, states, chains, base, n_shared):
    def wave(w):
        return [(c, base + c - w, w, w == 0) for c in range(w, chains)]

    def shared(step):
        j = jnp.maximum(base - 1 - step, 0)
        return [(c, j, c + 1 + step, False) for c in range(chains)]

    states = list(states)
    us = logits(wave(0))
    for w in range(chains):
        if w + 1 < chains:
            ahead = logits(wave(w + 1))
        elif n_shared is not None:
            ahead = logits(shared(0))
        new = finish(wave(w), us, states[w:])
        states[w:] = new
        us = ahead
    if n_shared is None:
        return states

    def body(step, carry):
        us, sts = carry
        ahead = logits(shared(step + 1))
        return tuple(ahead), tuple(finish(shared(step), list(us), list(sts)))

    return lax.fori_loop(0, n_shared, body, (tuple(us), tuple(states)))[1]


def _sb_head(qt, k_ref, vt_ref, o_ref, tile, blk, chains, single_tile):
    scale2 = HEAD_DIM ** -0.5 * LOG2_E
    key, qry = _block_iotas(blk)
    strict = key < qry
    suffix = jnp.where(qry > key, 1.0, 0.0).astype(_BF16)

    def gates(z2, diagonal):
        log_beta = jnp.minimum(z2, 0.0) - jnp.log2(1.0 + jnp.exp2(-jnp.abs(z2)))
        log_not_beta = log_beta - z2
        if diagonal:
            log_not_beta = jnp.where(strict, log_not_beta, 0.0)
        hi = log_not_beta.astype(_BF16)
        lo = (log_not_beta - hi.astype(_F32)).astype(_BF16)
        return log_beta, log_not_beta, hi, lo

    def logits(items):
        return [jnp.dot(_rows(k_ref, j, blk), qt[c], preferred_element_type=_F32) * scale2
                for c, j, _, _ in items]

    def finish(items, z2, states):
        g = [gates(z, diag) for z, (_, _, _, diag) in zip(z2, items)]
        after = [jnp.dot(suffix, hi, preferred_element_type=_F32)
                 + jnp.dot(suffix, lo, preferred_element_type=_F32) + carry
                 for (_, _, hi, lo), (carry, _) in zip(g, states)]
        new = []
        for (log_beta, log_not_beta, _, _), aft, (carry, acc), (_, j, _, diag) in zip(
                g, after, states, items):
            w = jnp.exp2(log_beta + aft)
            if diag:
                w = jnp.where(strict, w, 0.0)
            acc = acc + jnp.dot(_cols(vt_ref, j, blk), w.astype(_BF16),
                                preferred_element_type=_F32)
            new.append((carry + jnp.sum(log_not_beta, axis=0, keepdims=True), acc))
        return new

    init = (jnp.zeros((1, blk), _F32), jnp.zeros((HEAD_DIM, blk), _F32))
    base = tile * chains
    states = _chain_sweep(logits, finish, [init] * chains, chains, base,
                          None if single_tile else base)
    for c in range(chains):
        o_ref[c * blk:(c + 1) * blk, :] = jnp.transpose(states[c][1].astype(o_ref.dtype))


def _softmax_finish(values, shifts=None):
    def finish(items, us, states):
        mids = []
        for u, (c, _, _, _), (m, l, _) in zip(us, items, states):
            u_max = jnp.max(u, axis=0, keepdims=True)
            shift = None if shifts is None else shifts[c]
            if shift is not None:
                u_max = u_max + shift
            m_new = jnp.maximum(m, u_max)
            alpha = jnp.exp2(m - m_new)
            p = jnp.exp2(u - m_new if shift is None else u + (shift - m_new))
            mids.append((m_new, alpha, alpha * l + jnp.sum(p, axis=0, keepdims=True),
                         p.astype(_BF16)))
        return [(m_new, l_new, alpha * acc + jnp.dot(values(j), p, preferred_element_type=_F32))
                for (m_new, alpha, l_new, p), (_, j, _, _), (_, _, acc) in zip(mids, items, states)]
    return finish


def _softmax_init(blk):
    return (jnp.full((1, blk), NEG_INF, _F32), jnp.zeros((1, blk), _F32),
            jnp.zeros((HEAD_DIM, blk), _F32))


def _softmax_store(o_ref, states, blk):
    for c, (_, l, acc) in enumerate(states):
        o_ref[c * blk:(c + 1) * blk, :] = jnp.transpose((acc / l).astype(o_ref.dtype))


FOX_BIAS_PARTS = 3


def _fox_augment_keys(k_ref, cfc_ref, hh, kaug_ref):
    seq = k_ref.shape[0]
    lane = lax.broadcasted_iota(jnp.int32, (seq, cfc_ref.shape[2]), 1)
    cf = jnp.sum(jnp.where(lane == hh, cfc_ref[0], 0.0), axis=1, keepdims=True)
    parts = _split3(cf * -(HEAD_DIM ** 0.5))
    lane = lax.broadcasted_iota(jnp.int32, (seq, HEAD_DIM), 1)
    extra = jnp.zeros((seq, HEAD_DIM), _F32)
    for i, part in enumerate(parts):
        extra = jnp.where(lane == i, part.astype(_F32), extra)
    kaug_ref[:, :HEAD_DIM] = k_ref[...]
    kaug_ref[:, HEAD_DIM:] = extra.astype(_BF16)


def _fox_head(qt, kaug_ref, vt_ref, cfr_ref, o_ref, hh, tile, blk, chains, single_tile):
    scale2 = HEAD_DIM ** -0.5 * LOG2_E
    key, qry = _block_iotas(blk)
    causal = key <= qry
    sub = lax.broadcasted_iota(jnp.int32, (HEAD_DIM, blk), 0)
    ones_rows = jnp.where(sub < FOX_BIAS_PARTS, 1.0, 0.0).astype(_BF16)
    base = tile * chains
    qt_aug = [jnp.concatenate([qt[c], ones_rows], axis=0) for c in range(chains)]
    cf_q = [cfr_ref[0, hh, pl.ds(base + c, 1), :] * LOG2_E for c in range(chains)]

    def logit(c, j, dist, diagonal):
        u = jnp.dot(_rows(kaug_ref, j, blk), qt_aug[c], preferred_element_type=_F32) * scale2
        return jnp.where(causal, u, NEG_INF) if diagonal else u

    states = _chain_sweep(lambda items: [logit(*it) for it in items],
                          _softmax_finish(lambda j: _cols(vt_ref, j, blk), cf_q),
                          [_softmax_init(blk)] * chains, chains, base,
                          None if single_tile else base)
    _softmax_store(o_ref, states, blk)


def _head_refs(gi, q_ref, k_ref, v_ref, o_ref):
    cols = pl.ds(pl.multiple_of(gi * HEAD_DIM, HEAD_DIM), HEAD_DIM)
    return q_ref.at[gi], k_ref.at[gi], v_ref.at[gi], o_ref.at[:, cols]


def _query_chains(q, blk, chains):
    qt_all = _transpose_bf16(q[...])
    return [qt_all[:, c * blk:(c + 1) * blk] for c in range(chains)]


def _attn_even_kernel(q_ref, k_ref, v_ref, cfc_ref, cfr_ref, o_ref, vt_ref, kaug_ref,
                      *, blk, chains, single_tile, group):
    first_head = pl.program_id(1) * group
    tile = pl.program_id(2)

    def head(gi, stick_breaking):
        q, k, v, o = _head_refs(gi, q_ref, k_ref, v_ref, o_ref)
        vt_ref[...] = _transpose_bf16(v[...])
        qt = _query_chains(q, blk, chains)
        if stick_breaking:
            _sb_head(qt, k, vt_ref, o, tile, blk, chains, single_tile)
        else:
            hh = first_head + gi - N_SB_HEADS
            _fox_augment_keys(k, cfc_ref, hh, kaug_ref)
            _fox_head(qt, kaug_ref, vt_ref, cfr_ref, o, hh, tile, blk, chains, single_tile)

    def heads(stick_breaking):
        def body(gi, carry):
            head(gi, stick_breaking)
            return carry
        lax.fori_loop(0, group, body, 0)

    pl.when(first_head < N_SB_HEADS)(functools.partial(heads, True))
    pl.when(first_head >= N_SB_HEADS)(functools.partial(heads, False))


def _attn_specs(batch, seq, n_heads, rows, group):
    assert n_heads % group == 0
    nq = seq // rows
    ng = n_heads // group
    q_spec = pl.BlockSpec((group, rows, HEAD_DIM), lambda b, h, i: (h, b * nq + i, 0))
    k_spec = pl.BlockSpec((group, seq, HEAD_DIM), lambda b, h, i: (ng + h, b, 0))
    v_spec = pl.BlockSpec((group, seq, HEAD_DIM), lambda b, h, i: (2 * ng + h, b, 0))
    o_spec = pl.BlockSpec((rows, group * HEAD_DIM), lambda b, h, i: (b * nq + i, h))
    return nq, ng, q_spec, k_spec, v_spec, o_spec


def _attn_even(qkv, cf_col, cf_row, batch, seq, *, blk=ATT_BLOCK, chains=ATT_CHAINS,
               group=ATT_HEAD_GROUP):
    n_heads = N_SB_HEADS + N_FOX_HEADS
    assert N_SB_HEADS % group == 0
    nq, ng, q_spec, k_spec, v_spec, o_spec = _attn_specs(batch, seq, n_heads, blk * chains, group)
    gate_lanes = cf_col.shape[2]
    nk = seq // blk
    cf_row = cf_row.reshape(batch, N_FOX_HEADS, nk, blk)
    return pl.pallas_call(
        functools.partial(_attn_even_kernel, blk=blk, chains=chains, single_tile=nq == 1,
                          group=group),
        grid=(batch, ng, nq),
        in_specs=[q_spec, k_spec, v_spec,
                  pl.BlockSpec((1, seq, gate_lanes), lambda b, h, i: (b, 0, 0)),
                  pl.BlockSpec((1, N_FOX_HEADS, nk, blk), lambda b, h, i: (b, 0, 0, 0))],
        out_specs=o_spec,
        out_shape=jax.ShapeDtypeStruct((batch * seq, n_heads * HEAD_DIM), _BF16),
        scratch_shapes=[pltpu.VMEM((HEAD_DIM, seq), _BF16),
                        pltpu.VMEM((seq, 2 * HEAD_DIM), _BF16)],
        compiler_params=pltpu.CompilerParams(
            dimension_semantics=("parallel", "parallel", "arbitrary")),
        name="attn_even",
    )(qkv, qkv, qkv, cf_col, cf_row)


def _dilated_bias_tables(blk):
    n_near = (DIL_W2 + blk - 1) // blk + 1
    assert blk % DIL_D3 == 0 and n_near * blk - (blk - 1) > DIL_W2
    d = jnp.arange(n_near + 1, dtype=jnp.int32)[:, None, None]
    s = jnp.arange(blk, dtype=jnp.int32)[None, :, None]
    t = jnp.arange(blk, dtype=jnp.int32)[None, None, :]
    delta = d * blk + t - s
    count = ((delta <= DIL_W1).astype(_F32)
             + ((delta % DIL_D2 == 0) & (delta <= DIL_W2)).astype(_F32)
             + (delta % DIL_D3 == 0).astype(_F32))
    count = jnp.where(delta >= 0, count, 0.0)
    return jnp.where(count > 0, jnp.log2(jnp.maximum(count, 1.0)), NEG_INF), n_near


def _attn_dilated_kernel(q_ref, k_ref, v_ref, bias_ref, o_ref, vt_ref, *, blk, chains,
                         n_near, single_tile, group):
    tile = pl.program_id(2)
    scale2 = HEAD_DIM ** -0.5 * LOG2_E
    base = tile * chains

    def head(gi, carry):
        q, k, v, o = _head_refs(gi, q_ref, k_ref, v_ref, o_ref)
        vt_ref[...] = _transpose_bf16(v[...])
        qt = _query_chains(q, blk, chains)

        def logit(c, j, dist, diagonal):
            table = min(dist, n_near) if isinstance(dist, int) else jnp.minimum(dist, n_near)
            return (jnp.dot(_rows(k, j, blk), qt[c], preferred_element_type=_F32) * scale2
                    + bias_ref[table])

        states = _chain_sweep(lambda items: [logit(*it) for it in items],
                              _softmax_finish(lambda j: _cols(vt_ref, j, blk)),
                              [_softmax_init(blk)] * chains, chains, base,
                              None if single_tile else base)
        _softmax_store(o, states, blk)
        return carry

    lax.fori_loop(0, group, head, 0)


def _attn_dilated(qkv, batch, seq, n_heads, *, blk=ATT_BLOCK, chains=ATT_CHAINS,
                  group=ATT_HEAD_GROUP):
    assert seq <= DIL_D3 * DIL_W1
    nq, ng, q_spec, k_spec, v_spec, o_spec = _attn_specs(batch, seq, n_heads, blk * chains, group)
    bias, n_near = _dilated_bias_tables(blk)
    return pl.pallas_call(
        functools.partial(_attn_dilated_kernel, blk=blk, chains=chains, n_near=n_near,
                          single_tile=nq == 1, group=group),
        grid=(batch, ng, nq),
        in_specs=[q_spec, k_spec, v_spec,
                  pl.BlockSpec(bias.shape, lambda b, h, i: (0, 0, 0))],
        out_specs=o_spec,
        out_shape=jax.ShapeDtypeStruct((batch * seq, n_heads * HEAD_DIM), _BF16),
        scratch_shapes=[pltpu.VMEM((HEAD_DIM, seq), _BF16)],
        compiler_params=pltpu.CompilerParams(
            dimension_semantics=("parallel", "parallel", "arbitrary")),
        name="attn_dilated",
    )(qkv, qkv, qkv, bias)


def _outproj_kernel(o_ref, w_ref, x_ref, y_ref):
    y_ref[...] = x_ref[...] + jnp.dot(o_ref[...], w_ref[...].astype(_BF16),
                                      preferred_element_type=_F32)


def _outproj(o, w, layer, x, *, tm=1024, tn=1024):
    m, d_in = o.shape
    d = w.shape[2]
    n_j = d // tn
    est = 2 * tm * d_in * 2 + 2 * d_in * tn * 4 + d_in * tn * 2 + 4 * tm * tn * 4
    return pl.pallas_call(
        _outproj_kernel,
        grid=(m // tm, d // tn),
        in_specs=[pl.BlockSpec((tm, d_in), lambda i, j: (i, 0)),
                  pl.BlockSpec((None, d_in, tn), lambda i, j: (layer, 0, _snake(i, j, n_j))),
                  pl.BlockSpec((tm, tn), lambda i, j: (i, _snake(i, j, n_j)))],
        out_specs=pl.BlockSpec((tm, tn), lambda i, j: (i, _snake(i, j, n_j))),
        out_shape=jax.ShapeDtypeStruct((m, d), _F32),
        compiler_params=pltpu.CompilerParams(
            dimension_semantics=("parallel", "parallel"),
            vmem_limit_bytes=_vmem_limit(est)),
        name="outproj",
    )(o, w, x)


def kernel(x, norm_g, ffn1_w_gate, ffn1_w_up, ffn1_w_down, ffn2_w_gate, ffn2_w_up,
           ffn2_w_down, even_w_in, even_b_forget, even_w_out, odd_w_qkv, odd_w_out,
           final_norm_g):
    batch, seq, d = x.shape
    depth = norm_g.shape[0]
    n_heads = d // HEAD_DIM
    assert n_heads == N_SB_HEADS + N_FOX_HEADS
    xf = x.reshape(batch * seq, d)
    for layer in range(depth):
        j = layer // 2
        xf = _ffn(xf, norm_g[layer, 0], ffn1_w_gate, ffn1_w_up, ffn1_w_down, layer)
        if layer % 2 == 0:
            w_gate = jnp.zeros((d, V7X_LANES), _BF16).at[:, :N_FOX_HEADS].set(
                even_w_in[j, :, 3 * d:].astype(_BF16))
            qkv, gate_pre = _proj(xf, norm_g[layer, 1], even_w_in, j, 3 * d, w_gate=w_gate)
            cf_col, cf_row = _gates(gate_pre, even_b_forget[j], batch, seq)
            o = _attn_even(qkv, cf_col, cf_row, batch, seq)
            xf = _outproj(o, even_w_out, j, xf)
        else:
            qkv = _proj(xf, norm_g[layer, 1], odd_w_qkv, j, 3 * d, rope_seq=seq,
                        rope_cols=2 * d)
            o = _attn_dilated(qkv, batch, seq, n_heads)
            xf = _outproj(o, odd_w_out, j, xf)
        last = layer == depth - 1
        xf = _ffn(xf, norm_g[layer, 2], ffn2_w_gate, ffn2_w_up, ffn2_w_down, layer,
                  final_norm_g if last else None)
    return xf.reshape(batch, seq, d)
```

```python
import functools

import jax
import jax.numpy as jnp
from jax import lax
from jax.experimental import pallas as pl
from jax.experimental.pallas import tpu as pltpu

_F32 = jnp.float32
_BF16 = jnp.bfloat16

HEAD_DIM = 128
N_SB_HEADS = 8
N_FOX_HEADS = 8
ROPE_DIMS = HEAD_DIM // 4
ROPE_THETA = 500000.0
RMS_EPS = 1e-6
NEG_INF = -1e30
LOG2_E = 1.4426950408889634
DIL_W1, DIL_W2, DIL_D2, DIL_D3 = 128, 512, 4, 16

V7X_LANES = 128
V7X_VMEM_BYTES = 64 * 1024 * 1024
V7X_SCOPED_VMEM_DEFAULT = 32 * 1024 * 1024
VMEM_CAP_BYTES = V7X_VMEM_BYTES * 15 // 16

FFN_RING = 3
FFN_AHEAD = FFN_RING - 1
PROJ_CHUNK = 256
ATT_BLOCK = 256
ATT_CHAINS = 8
ATT_HEAD_GROUP = 4


def _vmem_limit(block_bytes):
    return int(min(VMEM_CAP_BYTES, max(V7X_SCOPED_VMEM_DEFAULT, block_bytes * 5 // 4)))


def _rms(x, g):
    ms = jnp.mean(x * x, axis=-1, keepdims=True)
    return x * lax.rsqrt(ms + RMS_EPS) * g


def _log1p_exp_neg_abs(z):
    return jnp.log(1.0 + jnp.exp(-jnp.abs(z)))


def _ffn_kernel(*refs, final_norm, layer, n_f, tf):
    if final_norm:
        (x_ref, g_ref, wg_hbm, wu_hbm, wd_hbm, gf_ref, o_ref,
         n_ref, wg_buf, wu_buf, wd_buf, sem) = refs
    else:
        x_ref, g_ref, wg_hbm, wu_hbm, wd_hbm, o_ref, n_ref, wg_buf, wu_buf, wd_buf, sem = refs

    first = pl.program_id(0) * n_f
    total = pl.num_programs(0) * n_f

    def weight_copies(g):
        slot = g % FFN_RING
        cols = pl.ds(pl.multiple_of((g % n_f) * tf, tf), tf)
        return (pltpu.make_async_copy(wg_hbm.at[layer, :, cols], wg_buf.at[slot], sem.at[0, slot]),
                pltpu.make_async_copy(wu_hbm.at[layer, :, cols], wu_buf.at[slot], sem.at[1, slot]),
                pltpu.make_async_copy(wd_hbm.at[layer, cols, :], wd_buf.at[slot], sem.at[2, slot]))

    @pl.when(first == 0)
    def _():
        for g in range(FFN_AHEAD):
            for copy in weight_copies(g):
                copy.start()

    x = x_ref[...]
    n_ref[...] = _rms(x, g_ref[...]).astype(_BF16)
    o_ref[...] = x

    def step(j, carry):
        g = first + j
        slot = g % FFN_RING

        @pl.when(g + FFN_AHEAD < total)
        def _():
            for copy in weight_copies(g + FFN_AHEAD):
                copy.start()

        for copy in weight_copies(g):
            copy.wait()
        n = n_ref[...]
        gate = jnp.dot(n, wg_buf[slot].astype(_BF16), preferred_element_type=_F32)
        up = jnp.dot(n, wu_buf[slot].astype(_BF16), preferred_element_type=_F32)
        h = (gate * jax.nn.sigmoid(gate)) * (up * 0.5)
        o_ref[...] += jnp.dot(h.astype(_BF16), wd_buf[slot].astype(_BF16),
                              preferred_element_type=_F32)
        return carry

    lax.fori_loop(0, n_f, step, 0)
    if final_norm:
        o_ref[...] = _rms(o_ref[...], gf_ref[...])


def _ffn(x, g, wg, wu, wd, layer, final_g=None, *, tm=1024, tf=256):
    m, d = x.shape
    f = wg.shape[2]
    assert m % tm == 0 and f % tf == 0
    final_norm = final_g is not None
    hbm = pl.BlockSpec(memory_space=pl.ANY)
    in_specs = [pl.BlockSpec((tm, d), lambda i: (i, 0)), pl.BlockSpec((1, d), lambda i: (0, 0)),
                hbm, hbm, hbm]
    args = [x, g.reshape(1, d), wg, wu, wd]
    if final_norm:
        in_specs.append(pl.BlockSpec((1, d), lambda i: (0, 0)))
        args.append(final_g.reshape(1, d))
    est = (2 * (2 * tm * d * 4) + tm * d * 2 + FFN_RING * 3 * d * tf * 4 + 3 * d * tf * 2
           + 4 * tm * tf * 4)
    return pl.pallas_call(
        functools.partial(_ffn_kernel, final_norm=final_norm, layer=layer, n_f=f // tf, tf=tf),
        grid=(m // tm,),
        in_specs=in_specs,
        out_specs=pl.BlockSpec((tm, d), lambda i: (i, 0)),
        out_shape=jax.ShapeDtypeStruct((m, d), _F32),
        scratch_shapes=[pltpu.VMEM((tm, d), _BF16),
                        pltpu.VMEM((FFN_RING, d, tf), _F32), pltpu.VMEM((FFN_RING, d, tf), _F32),
                        pltpu.VMEM((FFN_RING, tf, d), _F32),
                        pltpu.SemaphoreType.DMA((3, FFN_RING))],
        compiler_params=pltpu.CompilerParams(
            dimension_semantics=("arbitrary",),
            vmem_limit_bytes=_vmem_limit(est)),
        name="ffn_final" if final_norm else "ffn",
    )(*args)


def _snake(i, j, n):
    return jnp.where(i % 2 == 0, j, n - 1 - j)


def _store_heads(o_ref, col, y):
    for c in range(0, y.shape[1], HEAD_DIM):
        o_ref[(col + c) // HEAD_DIM] = y[:, c:c + HEAD_DIM].astype(o_ref.dtype)


def _proj_gate_kernel(x_ref, g_ref, w_ref, wf_ref, o_ref, gate_ref, n_ref):
    j = pl.program_id(1)

    @pl.when(j == 0)
    def _():
        n = _rms(x_ref[...], g_ref[...]).astype(_BF16)
        n_ref[...] = n
        gate_ref[...] = jnp.dot(n, wf_ref[...], preferred_element_type=_F32)

    for c in range(0, w_ref.shape[1], PROJ_CHUNK):
        y = jnp.dot(n_ref[...], w_ref[:, c:c + PROJ_CHUNK].astype(_BF16),
                    preferred_element_type=_F32)
        _store_heads(o_ref, c, y)


def _proj_rope_kernel(x_ref, g_ref, w_ref, cos_ref, sin_lo_ref, sin_hi_ref, o_ref, n_ref,
                      *, rope_blocks):
    @pl.when(pl.program_id(1) == 0)
    def _():
        n_ref[...] = _rms(x_ref[...], g_ref[...]).astype(_BF16)

    j = _snake(pl.program_id(0), pl.program_id(1), pl.num_programs(1))

    half = ROPE_DIMS // 2
    chunks = range(0, w_ref.shape[1], PROJ_CHUNK)

    def project(col):
        return jnp.dot(n_ref[...], w_ref[:, col:col + PROJ_CHUNK].astype(_BF16),
                       preferred_element_type=_F32)

    @pl.when(j < rope_blocks)
    def _():
        reps = PROJ_CHUNK // HEAD_DIM
        cos, sin_lo, sin_hi = (jnp.tile(t[...], (1, reps))
                               for t in (cos_ref, sin_lo_ref, sin_hi_ref))
        for col in chunks:
            y = project(col)
            _store_heads(o_ref, col, y * cos + pltpu.roll(y, PROJ_CHUNK - half, 1) * sin_lo
                         + pltpu.roll(y, half, 1) * sin_hi)

    @pl.when(j >= rope_blocks)
    def _():
        for col in chunks:
            _store_heads(o_ref, col, project(col))


def _rope_tables(seq):
    half = ROPE_DIMS // 2
    freqs = ROPE_THETA ** (-jnp.arange(half, dtype=_F32) / half)
    ang = jnp.arange(seq, dtype=_F32)[:, None] * freqs[None, :]
    cos, sin = jnp.cos(ang), jnp.sin(ang)
    pad = jnp.zeros((seq, HEAD_DIM - ROPE_DIMS), _F32)
    zero = jnp.zeros((seq, half), _F32)
    cos_t = jnp.concatenate([cos, cos, pad + 1.0], axis=1)
    sin_lo = jnp.concatenate([-sin, zero, pad], axis=1)
    sin_hi = jnp.concatenate([zero, sin, pad], axis=1)
    return cos_t, sin_lo, sin_hi


def _proj(x, g, w, layer, n_out, *, w_gate=None, rope_seq=None, rope_cols=0, tm=1024,
          tn=1024):
    m, d = x.shape
    assert m % tm == 0 and n_out % tn == 0 and n_out <= w.shape[2] and tn % PROJ_CHUNK == 0
    n_j = n_out // tn
    in_specs = [
        pl.BlockSpec((tm, d), lambda i, j: (i, 0)),
        pl.BlockSpec((1, d), lambda i, j: (0, 0)),
        pl.BlockSpec((d, tn), lambda i, j: (layer, _snake(i, j, n_j))),
    ]
    args = [x, g.reshape(1, d), w.reshape(-1, w.shape[2])]
    out_specs = pl.BlockSpec((tn // HEAD_DIM, tm, HEAD_DIM),
                             lambda i, j: (_snake(i, j, n_j), i, 0))
    out_shape = jax.ShapeDtypeStruct((n_out // HEAD_DIM, m, HEAD_DIM), _BF16)
    est = (2 * tm * d * 4 + tm * d * 2 + 2 * d * tn * 4 + d * tn * 2 + 2 * tm * tn * 2
           + 2 * tm * tn * 4)
    if w_gate is not None:
        ng = w_gate.shape[1]
        in_specs.append(pl.BlockSpec((d, ng), lambda i, j: (0, 0)))
        args.append(w_gate)
        out_specs = (out_specs, pl.BlockSpec((tm, ng), lambda i, j: (i, 0)))
        out_shape = (out_shape, jax.ShapeDtypeStruct((m, ng), _F32))
        body = _proj_gate_kernel
        name = "proj_gate"
    else:
        assert rope_seq % tm == 0 and rope_cols % tn == 0
        seq_blocks = rope_seq // tm
        tab_spec = pl.BlockSpec((tm, HEAD_DIM), lambda i, j: (i % seq_blocks, 0))
        in_specs += [tab_spec] * 3
        args += list(_rope_tables(rope_seq))
        est += 3 * 2 * tm * HEAD_DIM * 4
        body = functools.partial(_proj_rope_kernel, rope_blocks=rope_cols // tn)
        name = "proj_rope"
    return pl.pallas_call(
        body,
        grid=(m // tm, n_out // tn),
        in_specs=in_specs,
        out_specs=out_specs,
        out_shape=out_shape,
        scratch_shapes=[pltpu.VMEM((tm, d), _BF16)],
        compiler_params=pltpu.CompilerParams(
            dimension_semantics=("parallel", "arbitrary"),
            vmem_limit_bytes=_vmem_limit(est)),
        name=name,
    )(*args)


def _split3(x):
    hi = x.astype(_BF16)
    r = x - hi.astype(_F32)
    mid = r.astype(_BF16)
    lo = (r - mid.astype(_F32)).astype(_BF16)
    return hi, mid, lo


def _gates_kernel(gp_ref, b_ref, col_ref, row_ref, *, blk):
    seq = gp_ref.shape[1]
    z = gp_ref[0] + b_ref[...]
    log_f = jnp.minimum(z, 0.0) - _log1p_exp_neg_abs(z)
    r = lax.broadcasted_iota(jnp.int32, (blk, blk), 0)
    c = lax.broadcasted_iota(jnp.int32, (blk, blk), 1)
    tri = jnp.where(c <= r, 1.0, 0.0).astype(_BF16)
    carry = jnp.zeros((1, log_f.shape[1]), _F32)
    for s in range(seq // blk):
        part = log_f[s * blk:(s + 1) * blk]
        cs = carry
        for p in _split3(part):
            cs = cs + jnp.dot(tri, p, preferred_element_type=_F32)
        col_ref[0, s * blk:(s + 1) * blk, :] = cs
        carry = cs[blk - 1:blk, :]
    row_ref[0] = jnp.transpose(col_ref[0])[:row_ref.shape[1], :]


def _gates(gate_pre, b_forget, batch, seq):
    ng = gate_pre.shape[1]
    nh = b_forget.shape[0]
    b_pad = jnp.zeros((1, ng), _F32).at[0, :nh].set(b_forget)
    return pl.pallas_call(
        functools.partial(_gates_kernel, blk=ATT_BLOCK),
        grid=(batch,),
        in_specs=[pl.BlockSpec((1, seq, ng), lambda b: (b, 0, 0)),
                  pl.BlockSpec((1, ng), lambda b: (0, 0))],
        out_specs=(pl.BlockSpec((1, seq, ng), lambda b: (b, 0, 0)),
                   pl.BlockSpec((1, nh, seq), lambda b: (b, 0, 0))),
        out_shape=(jax.ShapeDtypeStruct((batch, seq, ng), _F32),
                   jax.ShapeDtypeStruct((batch, nh, seq), _F32)),
        compiler_params=pltpu.CompilerParams(dimension_semantics=("parallel",)),
        name="gates",
    )(gate_pre.reshape(batch, seq, ng), b_pad)


def _block_iotas(blk):
    key = lax.broadcasted_iota(jnp.int32, (blk, blk), 0)
    qry = lax.broadcasted_iota(jnp.int32, (blk, blk), 1)
    return key, qry


def _rows(ref, j, blk):
    return ref[pl.ds(pl.multiple_of(j * blk, blk), blk), :]


def _cols(ref, j, blk):
    return ref[:, pl.ds(pl.multiple_of(j * blk, blk), blk)]


def _chain_sweep(logits, finish, states, chains, base, n_shared):
    def wave(w):
        return [(c, base + c - w, w, w == 0) for c in range(w, chains)]

    def shared(step):
        j = jnp.maximum(base - 1 - step, 0)
        return [(c, j, c + 1 + step, False) for c in range(chains)]

    states = list(states)
    us = logits(wave(0))
    for w in range(chains):
        if w + 1 < chains:
            ahead = logits(wave(w + 1))
        elif n_shared is not None:
            ahead = logits(shared(0))
        new = finish(wave(w), us, states[w:])
        states[w:] = new
        us = ahead
    if n_shared is None:
        return states

    def body(step, carry):
        us, sts = carry
        ahead = logits(shared(step + 1))
        return tuple(ahead), tuple(finish(shared(step), list(us), list(sts)))

    return lax.fori_loop(0, n_shared, body, (tuple(us), tuple(states)))[1]


def _sb_head(qt, k_ref, vt_ref, o_ref, tile, blk, chains, single_tile):
    scale2 = HEAD_DIM ** -0.5 * LOG2_E
    key, qry = _block_iotas(blk)
    strict = key < qry
    suffix = jnp.where(qry > key, 1.0, 0.0).astype(_BF16)

    def gates(z2, diagonal):
        log_beta = jnp.minimum(z2, 0.0) - jnp.log2(1.0 + jnp.exp2(-jnp.abs(z2)))
        log_not_beta = log_beta - z2
        if diagonal:
            log_not_beta = jnp.where(strict, log_not_beta, 0.0)
        hi = log_not_beta.astype(_BF16)
        lo = (log_not_beta - hi.astype(_F32)).astype(_BF16)
        return log_beta, log_not_beta, hi, lo

    def logits(items):
        return [jnp.dot(_rows(k_ref, j, blk), qt[c], preferred_element_type=_F32) * scale2
                for c, j, _, _ in items]

    def finish(items, z2, states):
        g = [gates(z, diag) for z, (_, _, _, diag) in zip(z2, items)]
        after = [jnp.dot(suffix, hi, preferred_element_type=_F32)
                 + jnp.dot(suffix, lo, preferred_element_type=_F32) + carry
                 for (_, _, hi, lo), (carry, _) in zip(g, states)]
        new = []
        for (log_beta, log_not_beta, _, _), aft, (carry, acc), (_, j, _, diag) in zip(
                g, after, states, items):
            w = jnp.exp2(log_beta + aft)
            if diag:
                w = jnp.where(strict, w, 0.0)
            acc = acc + jnp.dot(_cols(vt_ref, j, blk), w.astype(_BF16),
                                preferred_element_type=_F32)
            new.append((carry + jnp.sum(log_not_beta, axis=0, keepdims=True), acc))
        return new

    init = (jnp.zeros((1, blk), _F32), jnp.zeros((HEAD_DIM, blk), _F32))
    base = tile * chains
    states = _chain_sweep(logits, finish, [init] * chains, chains, base,
                          None if single_tile else base)
    for c in range(chains):
        o_ref[c * blk:(c + 1) * blk, :] = jnp.transpose(states[c][1].astype(o_ref.dtype))


def _softmax_finish(values, shifts=None):
    def finish(items, us, states):
        mids = []
        for u, (c, _, _, _), (m, l, _) in zip(us, items, states):
            u_max = jnp.max(u, axis=0, keepdims=True)
            shift = None if shifts is None else shifts[c]
            if shift is not None:
                u_max = u_max + shift
            m_new = jnp.maximum(m, u_max)
            alpha = jnp.exp2(m - m_new)
            p = jnp.exp2(u - m_new if shift is None else u + (shift - m_new))
            mids.append((m_new, alpha, alpha * l + jnp.sum(p, axis=0, keepdims=True),
                         p.astype(_BF16)))
        return [(m_new, l_new, alpha * acc + jnp.dot(values(j), p, preferred_element_type=_F32))
                for (m_new, alpha, l_new, p), (_, j, _, _), (_, _, acc) in zip(mids, items, states)]
    return finish


def _softmax_init(blk):
    return (jnp.full((1, blk), NEG_INF, _F32), jnp.zeros((1, blk), _F32),
            jnp.zeros((HEAD_DIM, blk), _F32))


def _softmax_store(o_ref, states, blk):
    for c, (_, l, acc) in enumerate(states):
        o_ref[c * blk:(c + 1) * blk, :] = jnp.transpose((acc / l).astype(o_ref.dtype))


FOX_BIAS_PARTS = 3


def _fox_augment_keys(k_ref, cfc_ref, hh, kaug_ref):
    seq = k_ref.shape[0]
    lane = lax.broadcasted_iota(jnp.int32, (seq, cfc_ref.shape[2]), 1)
    cf = jnp.sum(jnp.where(lane == hh, cfc_ref[0], 0.0), axis=1, keepdims=True)
    parts = _split3(cf * -(HEAD_DIM ** 0.5))
    lane = lax.broadcasted_iota(jnp.int32, (seq, HEAD_DIM), 1)
    extra = jnp.zeros((seq, HEAD_DIM), _F32)
    for i, part in enumerate(parts):
        extra = jnp.where(lane == i, part.astype(_F32), extra)
    kaug_ref[:, :HEAD_DIM] = k_ref[...]
    kaug_ref[:, HEAD_DIM:] = extra.astype(_BF16)


def _fox_head(qt, kaug_ref, vt_ref, cfr_ref, o_ref, hh, tile, blk, chains, single_tile):
    scale2 = HEAD_DIM ** -0.5 * LOG2_E
    key, qry = _block_iotas(blk)
    causal = key <= qry
    sub = lax.broadcasted_iota(jnp.int32, (HEAD_DIM, blk), 0)
    ones_rows = jnp.where(sub < FOX_BIAS_PARTS, 1.0, 0.0).astype(_BF16)
    base = tile * chains
    qt_aug = [jnp.concatenate([qt[c], ones_rows], axis=0) for c in range(chains)]
    cf_q = [cfr_ref[0, hh, pl.ds(base + c, 1), :] * LOG2_E for c in range(chains)]

    def logit(c, j, dist, diagonal):
        u = jnp.dot(_rows(kaug_ref, j, blk), qt_aug[c], preferred_element_type=_F32) * scale2
        return jnp.where(causal, u, NEG_INF) if diagonal else u

    states = _chain_sweep(lambda items: [logit(*it) for it in items],
                          _softmax_finish(lambda j: _cols(vt_ref, j, blk), cf_q),
                          [_softmax_init(blk)] * chains, chains, base,
                          None if single_tile else base)
    _softmax_store(o_ref, states, blk)


def _head_refs(gi, q_ref, k_ref, v_ref, o_ref):
    cols = pl.ds(pl.multiple_of(gi * HEAD_DIM, HEAD_DIM), HEAD_DIM)
    return q_ref.at[gi], k_ref.at[gi], v_ref.at[gi], o_ref.at[:, cols]


def _query_chains(q, blk, chains):
    qt_all = jnp.transpose(q[...])
    return [qt_all[:, c * blk:(c + 1) * blk] for c in range(chains)]


def _attn_even_kernel(q_ref, k_ref, v_ref, cfc_ref, cfr_ref, o_ref, vt_ref, kaug_ref,
                      *, blk, chains, single_tile, group):
    first_head = pl.program_id(1) * group
    tile = pl.program_id(2)

    def head(gi, stick_breaking):
        q, k, v, o = _head_refs(gi, q_ref, k_ref, v_ref, o_ref)
        vt_ref[...] = jnp.transpose(v[...])
        qt = _query_chains(q, blk, chains)
        if stick_breaking:
            _sb_head(qt, k, vt_ref, o, tile, blk, chains, single_tile)
        else:
            hh = first_head + gi - N_SB_HEADS
            _fox_augment_keys(k, cfc_ref, hh, kaug_ref)
            _fox_head(qt, kaug_ref, vt_ref, cfr_ref, o, hh, tile, blk, chains, single_tile)

    def heads(stick_breaking):
        def body(gi, carry):
            head(gi, stick_breaking)
            return carry
        lax.fori_loop(0, group, body, 0)

    pl.when(first_head < N_SB_HEADS)(functools.partial(heads, True))
    pl.when(first_head >= N_SB_HEADS)(functools.partial(heads, False))


def _attn_specs(batch, seq, n_heads, rows, group):
    assert n_heads % group == 0
    nq = seq // rows
    ng = n_heads // group
    q_spec = pl.BlockSpec((group, rows, HEAD_DIM), lambda b, h, i: (h, b * nq + i, 0))
    k_spec = pl.BlockSpec((group, seq, HEAD_DIM), lambda b, h, i: (ng + h, b, 0))
    v_spec = pl.BlockSpec((group, seq, HEAD_DIM), lambda b, h, i: (2 * ng + h, b, 0))
    o_spec = pl.BlockSpec((rows, group * HEAD_DIM), lambda b, h, i: (b * nq + i, h))
    return nq, ng, q_spec, k_spec, v_spec, o_spec


def _attn_even(qkv, cf_col, cf_row, batch, seq, *, blk=ATT_BLOCK, chains=ATT_CHAINS,
               group=ATT_HEAD_GROUP):
    n_heads = N_SB_HEADS + N_FOX_HEADS
    assert N_SB_HEADS % group == 0
    nq, ng, q_spec, k_spec, v_spec, o_spec = _attn_specs(batch, seq, n_heads, blk * chains, group)
    gate_lanes = cf_col.shape[2]
    nk = seq // blk
    cf_row = cf_row.reshape(batch, N_FOX_HEADS, nk, blk)
    return pl.pallas_call(
        functools.partial(_attn_even_kernel, blk=blk, chains=chains, single_tile=nq == 1,
                          group=group),
        grid=(batch, ng, nq),
        in_specs=[q_spec, k_spec, v_spec,
                  pl.BlockSpec((1, seq, gate_lanes), lambda b, h, i: (b, 0, 0)),
                  pl.BlockSpec((1, N_FOX_HEADS, nk, blk), lambda b, h, i: (b, 0, 0, 0))],
        out_specs=o_spec,
        out_shape=jax.ShapeDtypeStruct((batch * seq, n_heads * HEAD_DIM), _BF16),
        scratch_shapes=[pltpu.VMEM((HEAD_DIM, seq), _BF16),
                        pltpu.VMEM((seq, 2 * HEAD_DIM), _BF16)],
        compiler_params=pltpu.CompilerParams(
            dimension_semantics=("parallel", "parallel", "arbitrary")),
        name="attn_even",
    )(qkv, qkv, qkv, cf_col, cf_row)


def _dilated_bias_tables(blk):
    n_near = (DIL_W2 + blk - 1) // blk + 1
    assert blk % DIL_D3 == 0 and n_near * blk - (blk - 1) > DIL_W2
    d = jnp.arange(n_near + 1, dtype=jnp.int32)[:, None, None]
    s = jnp.arange(blk, dtype=jnp.int32)[None, :, None]
    t = jnp.arange(blk, dtype=jnp.int32)[None, None, :]
    delta = d * blk + t - s
    count = ((delta <= DIL_W1).astype(_F32)
             + ((delta % DIL_D2 == 0) & (delta <= DIL_W2)).astype(_F32)
             + (delta % DIL_D3 == 0).astype(_F32))
    count = jnp.where(delta >= 0, count, 0.0)
    return jnp.where(count > 0, jnp.log2(jnp.maximum(count, 1.0)), NEG_INF), n_near


def _attn_dilated_kernel(q_ref, k_ref, v_ref, bias_ref, o_ref, vt_ref, *, blk, chains,
                         n_near, single_tile, group):
    tile = pl.program_id(2)
    scale2 = HEAD_DIM ** -0.5 * LOG2_E
    base = tile * chains

    def head(gi, carry):
        q, k, v, o = _head_refs(gi, q_ref, k_ref, v_ref, o_ref)
        vt_ref[...] = jnp.transpose(v[...])
        qt = _query_chains(q, blk, chains)

        def logit(c, j, dist, diagonal):
            table = min(dist, n_near) if isinstance(dist, int) else jnp.minimum(dist, n_near)
            return (jnp.dot(_rows(k, j, blk), qt[c], preferred_element_type=_F32) * scale2
                    + bias_ref[table])

        states = _chain_sweep(lambda items: [logit(*it) for it in items],
                              _softmax_finish(lambda j: _cols(vt_ref, j, blk)),
                              [_softmax_init(blk)] * chains, chains, base,
                              None if single_tile else base)
        _softmax_store(o, states, blk)
        return carry

    lax.fori_loop(0, group, head, 0)


def _attn_dilated(qkv, batch, seq, n_heads, *, blk=ATT_BLOCK, chains=ATT_CHAINS,
                  group=ATT_HEAD_GROUP):
    assert seq <= DIL_D3 * DIL_W1
    nq, ng, q_spec, k_spec, v_spec, o_spec = _attn_specs(batch, seq, n_heads, blk * chains, group)
    bias, n_near = _dilated_bias_tables(blk)
    return pl.pallas_call(
        functools.partial(_attn_dilated_kernel, blk=blk, chains=chains, n_near=n_near,
                          single_tile=nq == 1, group=group),
        grid=(batch, ng, nq),
        in_specs=[q_spec, k_spec, v_spec,
                  pl.BlockSpec(bias.shape, lambda b, h, i: (0, 0, 0))],
        out_specs=o_spec,
        out_shape=jax.ShapeDtypeStruct((batch * seq, n_heads * HEAD_DIM), _BF16),
        scratch_shapes=[pltpu.VMEM((HEAD_DIM, seq), _BF16)],
        compiler_params=pltpu.CompilerParams(
            dimension_semantics=("parallel", "parallel", "arbitrary")),
        name="attn_dilated",
    )(qkv, qkv, qkv, bias)


def _outproj_kernel(o_ref, w_ref, x_ref, y_ref):
    y_ref[...] = x_ref[...] + jnp.dot(o_ref[...], w_ref[...].astype(_BF16),
                                      preferred_element_type=_F32)


def _outproj(o, w, layer, x, *, tm=1024, tn=1024):
    m, d_in = o.shape
    d = w.shape[2]
    n_j = d // tn
    est = 2 * tm * d_in * 2 + 2 * d_in * tn * 4 + d_in * tn * 2 + 4 * tm * tn * 4
    return pl.pallas_call(
        _outproj_kernel,
        grid=(m // tm, d // tn),
        in_specs=[pl.BlockSpec((tm, d_in), lambda i, j: (i, 0)),
                  pl.BlockSpec((None, d_in, tn), lambda i, j: (layer, 0, _snake(i, j, n_j))),
                  pl.BlockSpec((tm, tn), lambda i, j: (i, _snake(i, j, n_j)))],
        out_specs=pl.BlockSpec((tm, tn), lambda i, j: (i, _snake(i, j, n_j))),
        out_shape=jax.ShapeDtypeStruct((m, d), _F32),
        compiler_params=pltpu.CompilerParams(
            dimension_semantics=("parallel", "parallel"),
            vmem_limit_bytes=_vmem_limit(est)),
        name="outproj",
    )(o, w, x)


def kernel(x, norm_g, ffn1_w_gate, ffn1_w_up, ffn1_w_down, ffn2_w_gate, ffn2_w_up,
           ffn2_w_down, even_w_in, even_b_forget, even_w_out, odd_w_qkv, odd_w_out,
           final_norm_g):
    batch, seq, d = x.shape
    depth = norm_g.shape[0]
    n_heads = d // HEAD_DIM
    assert n_heads == N_SB_HEADS + N_FOX_HEADS
    xf = x.reshape(batch * seq, d)
    for layer in range(depth):
        j = layer // 2
        xf = _ffn(xf, norm_g[layer, 0], ffn1_w_gate, ffn1_w_up, ffn1_w_down, layer)
        if layer % 2 == 0:
            w_gate = jnp.zeros((d, V7X_LANES), _BF16).at[:, :N_FOX_HEADS].set(
                even_w_in[j, :, 3 * d:].astype(_BF16))
            qkv, gate_pre = _proj(xf, norm_g[layer, 1], even_w_in, j, 3 * d, w_gate=w_gate)
            cf_col, cf_row = _gates(gate_pre, even_b_forget[j], batch, seq)
            o = _attn_even(qkv, cf_col, cf_row, batch, seq)
            xf = _outproj(o, even_w_out, j, xf)
        else:
            qkv = _proj(xf, norm_g[layer, 1], odd_w_qkv, j, 3 * d, rope_seq=seq,
                        rope_cols=2 * d)
            o = _attn_dilated(qkv, batch, seq, n_heads)
            xf = _outproj(o, odd_w_out, j, xf)
        last = layer == depth - 1
        xf = _ffn(xf, norm_g[layer, 2], ffn2_w_gate, ffn2_w_up, ffn2_w_down, layer,
                  final_norm_g if last else None)
    return xf.reshape(batch, seq, d)
```

```python
import functools

import jax
import jax.numpy as jnp
from jax import lax
from jax.experimental import pallas as pl
from jax.experimental.pallas import tpu as pltpu

_F32 = jnp.float32
_BF16 = jnp.bfloat16

HEAD_DIM = 128
N_SB_HEADS = 8
N_FOX_HEADS = 8
ROPE_DIMS = HEAD_DIM // 4
ROPE_THETA = 500000.0
RMS_EPS = 1e-6
NEG_INF = -1e30
LOG2_E = 1.4426950408889634
DIL_W1, DIL_W2, DIL_D2, DIL_D3 = 128, 512, 4, 16

V7X_LANES = 128
V7X_VMEM_BYTES = 64 * 1024 * 1024
V7X_SCOPED_VMEM_DEFAULT = 32 * 1024 * 1024
VMEM_CAP_BYTES = V7X_VMEM_BYTES * 15 // 16

FFN_RING = 3
FFN_AHEAD = FFN_RING - 1
PROJ_CHUNK = 256
ATT_BLOCK = 256
ATT_CHAINS = 8
ATT_HEAD_GROUP = 4


def _vmem_limit(block_bytes):
    return int(min(VMEM_CAP_BYTES, max(V7X_SCOPED_VMEM_DEFAULT, block_bytes * 5 // 4)))


def _rms(x, g):
    ms = jnp.mean(x * x, axis=-1, keepdims=True)
    return x * lax.rsqrt(ms + RMS_EPS) * g


def _log1p_exp_neg_abs(z):
    return jnp.log(1.0 + jnp.exp(-jnp.abs(z)))


def _ffn_kernel(*refs, final_norm, layer, n_f, tf):
    if final_norm:
        (x_ref, g_ref, wg_hbm, wu_hbm, wd_hbm, gf_ref, o_ref,
         n_ref, wg_buf, wu_buf, wd_buf, sem) = refs
    else:
        x_ref, g_ref, wg_hbm, wu_hbm, wd_hbm, o_ref, n_ref, wg_buf, wu_buf, wd_buf, sem = refs

    first = pl.program_id(0) * n_f
    total = pl.num_programs(0) * n_f

    def weight_copies(g):
        slot = g % FFN_RING
        cols = pl.ds(pl.multiple_of((g % n_f) * tf, tf), tf)
        return (pltpu.make_async_copy(wg_hbm.at[layer, :, cols], wg_buf.at[slot], sem.at[0, slot]),
                pltpu.make_async_copy(wu_hbm.at[layer, :, cols], wu_buf.at[slot], sem.at[1, slot]),
                pltpu.make_async_copy(wd_hbm.at[layer, cols, :], wd_buf.at[slot], sem.at[2, slot]))

    @pl.when(first == 0)
    def _():
        for g in range(FFN_AHEAD):
            for copy in weight_copies(g):
                copy.start()

    x = x_ref[...]
    n_ref[...] = _rms(x, g_ref[...]).astype(_BF16)
    o_ref[...] = x

    def step(j, carry):
        g = first + j
        slot = g % FFN_RING

        @pl.when(g + FFN_AHEAD < total)
        def _():
            for copy in weight_copies(g + FFN_AHEAD):
                copy.start()

        for copy in weight_copies(g):
            copy.wait()
        n = n_ref[...]
        gate = jnp.dot(n, wg_buf[slot].astype(_BF16), preferred_element_type=_F32)
        up = jnp.dot(n, wu_buf[slot].astype(_BF16), preferred_element_type=_F32)
        h = (gate * jax.nn.sigmoid(gate)) * (up * 0.5)
        o_ref[...] += jnp.dot(h.astype(_BF16), wd_buf[slot].astype(_BF16),
                              preferred_element_type=_F32)
        return carry

    lax.fori_loop(0, n_f, step, 0)
    if final_norm:
        o_ref[...] = _rms(o_ref[...], gf_ref[...])


def _ffn(x, g, wg, wu, wd, layer, final_g=None, *, tm=1024, tf=256):
    m, d = x.shape
    f = wg.shape[2]
    assert m % tm == 0 and f % tf == 0
    final_norm = final_g is not None
    hbm = pl.BlockSpec(memory_space=pl.ANY)
    in_specs = [pl.BlockSpec((tm, d), lambda i: (i, 0)), pl.BlockSpec((1, d), lambda i: (0, 0)),
                hbm, hbm, hbm]
    args = [x, g.reshape(1, d), wg, wu, wd]
    if final_norm:
        in_specs.append(pl.BlockSpec((1, d), lambda i: (0, 0)))
        args.append(final_g.reshape(1, d))
    est = (2 * (2 * tm * d * 4) + tm * d * 2 + FFN_RING * 3 * d * tf * 4 + 3 * d * tf * 2
           + 4 * tm * tf * 4)
    return pl.pallas_call(
        functools.partial(_ffn_kernel, final_norm=final_norm, layer=layer, n_f=f // tf, tf=tf),
        grid=(m // tm,),
        in_specs=in_specs,
        out_specs=pl.BlockSpec((tm, d), lambda i: (i, 0)),
        out_shape=jax.ShapeDtypeStruct((m, d), _F32),
        scratch_shapes=[pltpu.VMEM((tm, d), _BF16),
                        pltpu.VMEM((FFN_RING, d, tf), _F32), pltpu.VMEM((FFN_RING, d, tf), _F32),
                        pltpu.VMEM((FFN_RING, tf, d), _F32),
                        pltpu.SemaphoreType.DMA((3, FFN_RING))],
        compiler_params=pltpu.CompilerParams(
            dimension_semantics=("arbitrary",),
            vmem_limit_bytes=_vmem_limit(est)),
        name="ffn_final" if final_norm else "ffn",
    )(*args)


def _snake(i, j, n):
    return jnp.where(i % 2 == 0, j, n - 1 - j)


def _store_heads(o_ref, col, y):
    for c in range(0, y.shape[1], HEAD_DIM):
        o_ref[(col + c) // HEAD_DIM] = y[:, c:c + HEAD_DIM].astype(o_ref.dtype)


def _proj_gate_kernel(x_ref, g_ref, w_ref, wf_ref, o_ref, gate_ref, n_ref):
    j = pl.program_id(1)

    @pl.when(j == 0)
    def _():
        n = _rms(x_ref[...], g_ref[...]).astype(_BF16)
        n_ref[...] = n
        gate_ref[...] = jnp.dot(n, wf_ref[...], preferred_element_type=_F32)

    for c in range(0, w_ref.shape[1], PROJ_CHUNK):
        y = jnp.dot(n_ref[...], w_ref[:, c:c + PROJ_CHUNK].astype(_BF16),
                    preferred_element_type=_F32)
        _store_heads(o_ref, c, y)


def _proj_rope_kernel(x_ref, g_ref, w_ref, cos_ref, sin_lo_ref, sin_hi_ref, o_ref, n_ref,
                      *, rope_blocks):
    @pl.when(pl.program_id(1) == 0)
    def _():
        n_ref[...] = _rms(x_ref[...], g_ref[...]).astype(_BF16)

    j = _snake(pl.program_id(0), pl.program_id(1), pl.num_programs(1))

    half = ROPE_DIMS // 2
    chunks = range(0, w_ref.shape[1], PROJ_CHUNK)

    def project(col):
        return jnp.dot(n_ref[...], w_ref[:, col:col + PROJ_CHUNK].astype(_BF16),
                       preferred_element_type=_F32)

    @pl.when(j < rope_blocks)
    def _():
        reps = PROJ_CHUNK // HEAD_DIM
        cos, sin_lo, sin_hi = (jnp.tile(t[...], (1, reps))
                               for t in (cos_ref, sin_lo_ref, sin_hi_ref))
        for col in chunks:
            y = project(col)
            _store_heads(o_ref, col, y * cos + pltpu.roll(y, PROJ_CHUNK - half, 1) * sin_lo
                         + pltpu.roll(y, half, 1) * sin_hi)

    @pl.when(j >= rope_blocks)
    def _():
        for col in chunks:
            _store_heads(o_ref, col, project(col))


def _rope_tables(seq):
    half = ROPE_DIMS // 2
    freqs = ROPE_THETA ** (-jnp.arange(half, dtype=_F32) / half)
    ang = jnp.arange(seq, dtype=_F32)[:, None] * freqs[None, :]
    cos, sin = jnp.cos(ang), jnp.sin(ang)
    pad = jnp.zeros((seq, HEAD_DIM - ROPE_DIMS), _F32)
    zero = jnp.zeros((seq, half), _F32)
    cos_t = jnp.concatenate([cos, cos, pad + 1.0], axis=1)
    sin_lo = jnp.concatenate([-sin, zero, pad], axis=1)
    sin_hi = jnp.concatenate([zero, sin, pad], axis=1)
    return cos_t, sin_lo, sin_hi


def _proj(x, g, w, layer, n_out, *, w_gate=None, rope_seq=None, rope_cols=0, tm=1024,
          tn=1024):
    m, d = x.shape
    assert m % tm == 0 and n_out % tn == 0 and n_out <= w.shape[2] and tn % PROJ_CHUNK == 0
    n_j = n_out // tn
    in_specs = [
        pl.BlockSpec((tm, d), lambda i, j: (i, 0)),
        pl.BlockSpec((1, d), lambda i, j: (0, 0)),
        pl.BlockSpec((d, tn), lambda i, j: (layer, _snake(i, j, n_j))),
    ]
    args = [x, g.reshape(1, d), w.reshape(-1, w.shape[2])]
    out_specs = pl.BlockSpec((tn // HEAD_DIM, tm, HEAD_DIM),
                             lambda i, j: (_snake(i, j, n_j), i, 0))
    out_shape = jax.ShapeDtypeStruct((n_out // HEAD_DIM, m, HEAD_DIM), _BF16)
    est = (2 * tm * d * 4 + tm * d * 2 + 2 * d * tn * 4 + d * tn * 2 + 2 * tm * tn * 2
           + 2 * tm * tn * 4)
    if w_gate is not None:
        ng = w_gate.shape[1]
        in_specs.append(pl.BlockSpec((d, ng), lambda i, j: (0, 0)))
        args.append(w_gate)
        out_specs = (out_specs, pl.BlockSpec((tm, ng), lambda i, j: (i, 0)))
        out_shape = (out_shape, jax.ShapeDtypeStruct((m, ng), _F32))
        body = _proj_gate_kernel
        name = "proj_gate"
    else:
        assert rope_seq % tm == 0 and rope_cols % tn == 0
        seq_blocks = rope_seq // tm
        tab_spec = pl.BlockSpec((tm, HEAD_DIM), lambda i, j: (i % seq_blocks, 0))
        in_specs += [tab_spec] * 3
        args += list(_rope_tables(rope_seq))
        est += 3 * 2 * tm * HEAD_DIM * 4
        body = functools.partial(_proj_rope_kernel, rope_blocks=rope_cols // tn)
        name = "proj_rope"
    return pl.pallas_call(
        body,
        grid=(m // tm, n_out // tn),
        in_specs=in_specs,
        out_specs=out_specs,
        out_shape=out_shape,
        scratch_shapes=[pltpu.VMEM((tm, d), _BF16)],
        compiler_params=pltpu.CompilerParams(
            dimension_semantics=("parallel", "arbitrary"),
            vmem_limit_bytes=_vmem_limit(est)),
        name=name,
    )(*args)


def _split3(x):
    hi = x.astype(_BF16)
    r = x - hi.astype(_F32)
    mid = r.astype(_BF16)
    lo = (r - mid.astype(_F32)).astype(_BF16)
    return hi, mid, lo


def _gates_kernel(gp_ref, b_ref, col_ref, row_ref, *, blk):
    seq = gp_ref.shape[1]
    z = gp_ref[0] + b_ref[...]
    log_f = jnp.minimum(z, 0.0) - _log1p_exp_neg_abs(z)
    r = lax.broadcasted_iota(jnp.int32, (blk, blk), 0)
    c = lax.broadcasted_iota(jnp.int32, (blk, blk), 1)
    tri = jnp.where(c <= r, 1.0, 0.0).astype(_BF16)
    carry = jnp.zeros((1, log_f.shape[1]), _F32)
    for s in range(seq // blk):
        part = log_f[s * blk:(s + 1) * blk]
        cs = carry
        for p in _split3(part):
            cs = cs + jnp.dot(tri, p, preferred_element_type=_F32)
        col_ref[0, s * blk:(s + 1) * blk, :] = cs
        carry = cs[blk - 1:blk, :]
    row_ref[0] = jnp.transpose(col_ref[0])[:row_ref.shape[1], :]


def _gates(gate_pre, b_forget, batch, seq):
    ng = gate_pre.shape[1]
    nh = b_forget.shape[0]
    b_pad = jnp.zeros((1, ng), _F32).at[0, :nh].set(b_forget)
    return pl.pallas_call(
        functools.partial(_gates_kernel, blk=ATT_BLOCK),
        grid=(batch,),
        in_specs=[pl.BlockSpec((1, seq, ng), lambda b: (b, 0, 0)),
                  pl.BlockSpec((1, ng), lambda b: (0, 0))],
        out_specs=(pl.BlockSpec((1, seq, ng), lambda b: (b, 0, 0)),
                   pl.BlockSpec((1, nh, seq), lambda b: (b, 0, 0))),
        out_shape=(jax.ShapeDtypeStruct((batch, seq, ng), _F32),
                   jax.ShapeDtypeStruct((batch, nh, seq), _F32)),
        compiler_params=pltpu.CompilerParams(dimension_semantics=("parallel",)),
        name="gates",
    )(gate_pre.reshape(batch, seq, ng), b_pad)


def _block_iotas(blk):
    key = lax.broadcasted_iota(jnp.int32, (blk, blk), 0)
    qry = lax.broadcasted_iota(jnp.int32, (blk, blk), 1)
    return key, qry


def _rows(ref, j, blk):
    return ref[pl.ds(pl.multiple_of(j * blk, blk), blk), :]


def _cols(ref, j, blk):
    return ref[:, pl.ds(pl.multiple_of(j * blk, blk), blk)]


def _chain_sweep(logits, finish, states, chains, base, n_shared):
    def wave(w):
        return [(c, base + c - w, w, w == 0) for c in range(w, chains)]

    def shared(step):
        j = jnp.maximum(base - 1 - step, 0)
        return [(c, j, c + 1 + step, False) for c in range(chains)]

    states = list(states)
    us = logits(wave(0))
    for w in range(chains):
        if w + 1 < chains:
            ahead = logits(wave(w + 1))
        elif n_shared is not None:
            ahead = logits(shared(0))
        new = finish(wave(w), us, states[w:])
        states[w:] = new
        us = ahead
    if n_shared is None:
        return states

    def body(step, carry):
        us, sts = carry
        ahead = logits(shared(step + 1))
        return tuple(ahead), tuple(finish(shared(step), list(us), list(sts)))

    return lax.fori_loop(0, n_shared, body, (tuple(us), tuple(states)))[1]


def _sb_head(qt, k_ref, vt_ref, o_ref, tile, blk, chains, single_tile):
    scale2 = HEAD_DIM ** -0.5 * LOG2_E
    key, qry = _block_iotas(blk)
    strict = key < qry
    suffix = jnp.where(qry > key, 1.0, 0.0).astype(_BF16)

    def gates(z2, diagonal):
        log_beta = jnp.minimum(z2, 0.0) - jnp.log2(1.0 + jnp.exp2(-jnp.abs(z2)))
        log_not_beta = log_beta - z2
        if diagonal:
            log_not_beta = jnp.where(strict, log_not_beta, 0.0)
        hi = log_not_beta.astype(_BF16)
        lo = (log_not_beta - hi.astype(_F32)).astype(_BF16)
        return log_beta, log_not_beta, hi, lo

    def logits(items):
        return [jnp.dot(_rows(k_ref, j, blk), qt[c], preferred_element_type=_F32) * scale2
                for c, j, _, _ in items]

    def finish(items, z2, states):
        g = [gates(z, diag) for z, (_, _, _, diag) in zip(z2, items)]
        after = [jnp.dot(suffix, hi, preferred_element_type=_F32)
                 + jnp.dot(suffix, lo, preferred_element_type=_F32) + carry
                 for (_, _, hi, lo), (carry, _) in zip(g, states)]
        new = []
        for (log_beta, log_not_beta, _, _), aft, (carry, acc), (_, j, _, diag) in zip(
                g, after, states, items):
            w = jnp.exp2(log_beta + aft)
            if diag:
                w = jnp.where(strict, w, 0.0)
            acc = acc + jnp.dot(_cols(vt_ref, j, blk), w.astype(_BF16),
                                preferred_element_type=_F32)
            new.append((carry + jnp.sum(log_not_beta, axis=0, keepdims=True), acc))
        return new

    init = (jnp.zeros((1, blk), _F32), jnp.zeros((HEAD_DIM, blk), _F32))
    base = tile * chains
    states = _chain_sweep(logits, finish, [init] * chains, chains, base,
                          None if single_tile else base)
    for c in range(chains):
        o_ref[c * blk:(c + 1) * blk, :] = jnp.transpose(states[c][1].astype(o_ref.dtype))


def _softmax_finish(values, shifts=None):
    def finish(items, us, states):
        mids = []
        for u, (c, _, _, _), (m, l, _) in zip(us, items, states):
            u_max = jnp.max(u, axis=0, keepdims=True)
            shift = None if shifts is None else shifts[c]
            if shift is not None:
                u_max = u_max + shift
            m_new = jnp.maximum(m, u_max)
            alpha = jnp.exp2(m - m_new)
            p = jnp.exp2(u - m_new if shift is None else u + (shift - m_new))
            mids.append((m_new, alpha, alpha * l + jnp.sum(p, axis=0, keepdims=True),
                         p.astype(_BF16)))
        return [(m_new, l_new, alpha * acc + jnp.dot(values(j), p, preferred_element_type=_F32))
                for (m_new, alpha, l_new, p), (_, j, _, _), (_, _, acc) in zip(mids, items, states)]
    return finish


def _softmax_init(blk):
    return (jnp.full((1, blk), NEG_INF, _F32), jnp.zeros((1, blk), _F32),
            jnp.zeros((HEAD_DIM, blk), _F32))


def _softmax_store(o_ref, states, blk):
    for c, (_, l, acc) in enumerate(states):
        o_ref[c * blk:(c + 1) * blk, :] = jnp.transpose((acc / l).astype(o_ref.dtype))


FOX_BIAS_PARTS = 3


def _fox_augment_keys(k_ref, cfc_ref, hh, kaug_ref):
    seq = k_ref.shape[0]
    lane = lax.broadcasted_iota(jnp.int32, (seq, cfc_ref.shape[2]), 1)
    cf = jnp.sum(jnp.where(lane == hh, cfc_ref[0], 0.0), axis=1, keepdims=True)
    parts = _split3(cf * -(HEAD_DIM ** 0.5))
    lane = lax.broadcasted_iota(jnp.int32, (seq, HEAD_DIM), 1)
    extra = jnp.zeros((seq, HEAD_DIM), _F32)
    for i, part in enumerate(parts):
        extra = jnp.where(lane == i, part.astype(_F32), extra)
    kaug_ref[:, :HEAD_DIM] = k_ref[...]
    kaug_ref[:, HEAD_DIM:] = extra.astype(_BF16)


def _fox_head(qt, kaug_ref, vt_ref, cfr_ref, o_ref, hh, tile, blk, chains, single_tile):
    scale2 = HEAD_DIM ** -0.5 * LOG2_E
    key, qry = _block_iotas(blk)
    causal = key <= qry
    sub = lax.broadcasted_iota(jnp.int32, (HEAD_DIM, blk), 0)
    ones_rows = jnp.where(sub < FOX_BIAS_PARTS, 1.0, 0.0).astype(_BF16)
    base = tile * chains
    qt_aug = [jnp.concatenate([qt[c], ones_rows], axis=0) for c in range(chains)]
    cf_q = [cfr_ref[0, hh, pl.ds(base + c, 1), :] * LOG2_E for c in range(chains)]

    def logit(c, j, dist, diagonal):
        u = jnp.dot(_rows(kaug_ref, j, blk), qt_aug[c], preferred_element_type=_F32) * scale2
        return jnp.where(causal, u, NEG_INF) if diagonal else u

    states = _chain_sweep(lambda items: [logit(*it) for it in items],
                          _softmax_finish(lambda j: _cols(vt_ref, j, blk), cf_q),
                          [_softmax_init(blk)] * chains, chains, base,
                          None if single_tile else base)
    _softmax_store(o_ref, states, blk)


def _head_refs(gi, q_ref, k_ref, v_ref, o_ref):
    cols = pl.ds(pl.multiple_of(gi * HEAD_DIM, HEAD_DIM), HEAD_DIM)
    return q_ref.at[gi], k_ref.at[gi], v_ref.at[gi], o_ref.at[:, cols]


def _query_chains(q, blk, chains):
    qt_all = jnp.transpose(q[...])
    return [qt_all[:, c * blk:(c + 1) * blk] for c in range(chains)]


def _attn_even_kernel(q_ref, k_ref, v_ref, cfc_ref, cfr_ref, o_ref, vt_ref, kaug_ref,
                      *, blk, chains, single_tile, group):
    first_head = pl.program_id(1) * group
    tile = pl.program_id(2)

    def head(gi, stick_breaking):
        q, k, v, o = _head_refs(gi, q_ref, k_ref, v_ref, o_ref)
        vt_ref[...] = jnp.transpose(v[...])
        qt = _query_chains(q, blk, chains)
        if stick_breaking:
            _sb_head(qt, k, vt_ref, o, tile, blk, chains, single_tile)
        else:
            hh = first_head + gi - N_SB_HEADS
            _fox_augment_keys(k, cfc_ref, hh, kaug_ref)
            _fox_head(qt, kaug_ref, vt_ref, cfr_ref, o, hh, tile, blk, chains, single_tile)

    def heads(stick_breaking):
        def body(gi, carry):
            head(gi, stick_breaking)
            return carry
        lax.fori_loop(0, group, body, 0)

    pl.when(first_head < N_SB_HEADS)(functools.partial(heads, True))
    pl.when(first_head >= N_SB_HEADS)(functools.partial(heads, False))


def _attn_specs(batch, seq, n_heads, rows, group):
    assert n_heads % group == 0
    nq = seq // rows
    ng = n_heads // group
    q_spec = pl.BlockSpec((group, rows, HEAD_DIM), lambda b, h, i: (h, b * nq + i, 0))
    k_spec = pl.BlockSpec((group, seq, HEAD_DIM), lambda b, h, i: (ng + h, b, 0))
    v_spec = pl.BlockSpec((group, seq, HEAD_DIM), lambda b, h, i: (2 * ng + h, b, 0))
    o_spec = pl.BlockSpec((rows, group * HEAD_DIM), lambda b, h, i: (b * nq + i, h))
    return nq, ng, q_spec, k_spec, v_spec, o_spec


def _attn_even(qkv, cf_col, cf_row, batch, seq, *, blk=ATT_BLOCK, chains=ATT_CHAINS,
               group=ATT_HEAD_GROUP):
    n_heads = N_SB_HEADS + N_FOX_HEADS
    assert N_SB_HEADS % group == 0
    nq, ng, q_spec, k_spec, v_spec, o_spec = _attn_specs(batch, seq, n_heads, blk * chains, group)
    gate_lanes = cf_col.shape[2]
    nk = seq // blk
    cf_row = cf_row.reshape(batch, N_FOX_HEADS, nk, blk)
    return pl.pallas_call(
        functools.partial(_attn_even_kernel, blk=blk, chains=chains, single_tile=nq == 1,
                          group=group),
        grid=(batch, ng, nq),
        in_specs=[q_spec, k_spec, v_spec,
                  pl.BlockSpec((1, seq, gate_lanes), lambda b, h, i: (b, 0, 0)),
                  pl.BlockSpec((1, N_FOX_HEADS, nk, blk), lambda b, h, i: (b, 0, 0, 0))],
        out_specs=o_spec,
        out_shape=jax.ShapeDtypeStruct((batch * seq, n_heads * HEAD_DIM), _BF16),
        scratch_shapes=[pltpu.VMEM((HEAD_DIM, seq), _BF16),
                        pltpu.VMEM((seq, 2 * HEAD_DIM), _BF16)],
        compiler_params=pltpu.CompilerParams(
            dimension_semantics=("parallel", "parallel", "arbitrary")),
        name="attn_even",
    )(qkv, qkv, qkv, cf_col, cf_row)


def _dilated_bias_tables(blk):
    n_near = (DIL_W2 + blk - 1) // blk + 1
    assert blk % DIL_D3 == 0 and n_near * blk - (blk - 1) > DIL_W2
    d = jnp.arange(n_near + 1, dtype=jnp.int32)[:, None, None]
    s = jnp.arange(blk, dtype=jnp.int32)[None, :, None]
    t = jnp.arange(blk, dtype=jnp.int32)[None, None, :]
    delta = d * blk + t - s
    count = ((delta <= DIL_W1).astype(_F32)
             + ((delta % DIL_D2 == 0) & (delta <= DIL_W2)).astype(_F32)
             + (delta % DIL_D3 == 0).astype(_F32))
    count = jnp.where(delta >= 0, count, 0.0)
    return jnp.where(count > 0, jnp.log2(jnp.maximum(count, 1.0)), NEG_INF), n_near


def _attn_dilated_kernel(q_ref, k_ref, v_ref, bias_ref, o_ref, vt_ref, *, blk, chains,
                         n_near, single_tile, group):
    tile = pl.program_id(2)
    scale2 = HEAD_DIM ** -0.5 * LOG2_E
    base = tile * chains

    def head(gi, carry):
        q, k, v, o = _head_refs(gi, q_ref, k_ref, v_ref, o_ref)
        vt_ref[...] = jnp.transpose(v[...])
        qt = _query_chains(q, blk, chains)

        def logit(c, j, dist, diagonal):
            table = min(dist, n_near) if isinstance(dist, int) else jnp.minimum(dist, n_near)
            return (jnp.dot(_rows(k, j, blk), qt[c], preferred_element_type=_F32) * scale2
                    + bias_ref[table])

        states = _chain_sweep(lambda items: [logit(*it) for it in items],
                              _softmax_finish(lambda j: _cols(vt_ref, j, blk)),
                              [_softmax_init(blk)] * chains, chains, base,
                              None if single_tile else base)
        _softmax_store(o, states, blk)
        return carry

    lax.fori_loop(0, group, head, 0)


def _attn_dilated(qkv, batch, seq, n_heads, *, blk=ATT_BLOCK, chains=ATT_CHAINS,
                  group=ATT_HEAD_GROUP):
    assert seq <= DIL_D3 * DIL_W1
    nq, ng, q_spec, k_spec, v_spec, o_spec = _attn_specs(batch, seq, n_heads, blk * chains, group)
    bias, n_near = _dilated_bias_tables(blk)
    return pl.pallas_call(
        functools.partial(_attn_dilated_kernel, blk=blk, chains=chains, n_near=n_near,
                          single_tile=nq == 1, group=group),
        grid=(batch, ng, nq),
        in_specs=[q_spec, k_spec, v_spec,
                  pl.BlockSpec(bias.shape, lambda b, h, i: (0, 0, 0))],
        out_specs=o_spec,
        out_shape=jax.ShapeDtypeStruct((batch * seq, n_heads * HEAD_DIM), _BF16),
        scratch_shapes=[pltpu.VMEM((HEAD_DIM, seq), _BF16)],
        compiler_params=pltpu.CompilerParams(
            dimension_semantics=("parallel", "parallel", "arbitrary")),
        name="attn_dilated",
    )(qkv, qkv, qkv, bias)


def _outproj_kernel(o_ref, w_hbm, x_ref, y_ref, w_bf, stage, sem, *, layer, chunk):
    d = y_ref.shape[1]
    chunks = range(0, d, chunk)

    def weight_copy(c):
        slot = (c // chunk) % 2
        return pltpu.make_async_copy(w_hbm.at[layer, :, c:c + chunk], stage.at[slot], sem.at[slot])

    @pl.when(pl.program_id(0) == 0)
    def _():
        weight_copy(0).start()
        for c in chunks:
            if c + chunk < d:
                weight_copy(c + chunk).start()
            weight_copy(c).wait()
            w_bf[:, c:c + chunk] = stage[(c // chunk) % 2].astype(_BF16)

    for c in chunks:
        y_ref[:, c:c + chunk] = x_ref[:, c:c + chunk] + jnp.dot(
            o_ref[...], w_bf[:, c:c + chunk], preferred_element_type=_F32)


def _outproj(o, w, layer, x, *, tm=512, chunk=512):
    m, d_in = o.shape
    d = w.shape[2]
    assert m % tm == 0 and d % chunk == 0
    est = (2 * tm * d_in * 2 + 4 * tm * d * 4 + d_in * d * 2 + 2 * d_in * chunk * 4
           + tm * chunk * 4)
    return pl.pallas_call(
        functools.partial(_outproj_kernel, layer=layer, chunk=chunk),
        grid=(m // tm,),
        in_specs=[pl.BlockSpec((tm, d_in), lambda i: (i, 0)),
                  pl.BlockSpec(memory_space=pl.ANY),
                  pl.BlockSpec((tm, d), lambda i: (i, 0))],
        out_specs=pl.BlockSpec((tm, d), lambda i: (i, 0)),
        out_shape=jax.ShapeDtypeStruct((m, d), _F32),
        scratch_shapes=[pltpu.VMEM((d_in, d), _BF16), pltpu.VMEM((2, d_in, chunk), _F32),
                        pltpu.SemaphoreType.DMA((2,))],
        compiler_params=pltpu.CompilerParams(
            dimension_semantics=("arbitrary",),
            vmem_limit_bytes=_vmem_limit(est)),
        name="outproj",
    )(o, w, x)


def kernel(x, norm_g, ffn1_w_gate, ffn1_w_up, ffn1_w_down, ffn2_w_gate, ffn2_w_up,
           ffn2_w_down, even_w_in, even_b_forget, even_w_out, odd_w_qkv, odd_w_out,
           final_norm_g):
    batch, seq, d = x.shape
    depth = norm_g.shape[0]
    n_heads = d // HEAD_DIM
    assert n_heads == N_SB_HEADS + N_FOX_HEADS
    xf = x.reshape(batch * seq, d)
    for layer in range(depth):
        j = layer // 2
        xf = _ffn(xf, norm_g[layer, 0], ffn1_w_gate, ffn1_w_up, ffn1_w_down, layer)
        if layer % 2 == 0:
            w_gate = jnp.zeros((d, V7X_LANES), _BF16).at[:, :N_FOX_HEADS].set(
                even_w_in[j, :, 3 * d:].astype(_BF16))
            qkv, gate_pre = _proj(xf, norm_g[layer, 1], even_w_in, j, 3 * d, w_gate=w_gate)
            cf_col, cf_row = _gates(gate_pre, even_b_forget[j], batch, seq)
            o = _attn_even(qkv, cf_col, cf_row, batch, seq)
            xf = _outproj(o, even_w_out, j, xf)
        else:
            qkv = _proj(xf, norm_g[layer, 1], odd_w_qkv, j, 3 * d, rope_seq=seq,
                        rope_cols=2 * d)
            o = _attn_dilated(qkv, batch, seq, n_heads)
            xf = _outproj(o, odd_w_out, j, xf)
        last = layer == depth - 1
        xf = _ffn(xf, norm_g[layer, 2], ffn2_w_gate, ffn2_w_up, ffn2_w_down, layer,
                  final_norm_g if last else None)
    return xf.reshape(batch, seq, d)
```

```python
import functools

import jax
import jax.numpy as jnp
from jax import lax
from jax.experimental import pallas as pl
from jax.experimental.pallas import tpu as pltpu

_F32 = jnp.float32
_BF16 = jnp.bfloat16

HEAD_DIM = 128
N_SB_HEADS = 8
N_FOX_HEADS = 8
ROPE_DIMS = HEAD_DIM // 4
ROPE_THETA = 500000.0
RMS_EPS = 1e-6
NEG_INF = -1e30
LOG2_E = 1.4426950408889634
DIL_W1, DIL_W2, DIL_D2, DIL_D3 = 128, 512, 4, 16

V7X_LANES = 128
V7X_VMEM_BYTES = 64 * 1024 * 1024
V7X_SCOPED_VMEM_DEFAULT = 32 * 1024 * 1024
VMEM_CAP_BYTES = V7X_VMEM_BYTES * 15 // 16

FFN_RING = 3
FFN_AHEAD = FFN_RING - 1
PROJ_CHUNK = 256
ATT_BLOCK = 256
ATT_CHAINS = 8
ATT_HEAD_GROUP = 4


def _vmem_limit(block_bytes):
    return int(min(VMEM_CAP_BYTES, max(V7X_SCOPED_VMEM_DEFAULT, block_bytes * 5 // 4)))


def _rms(x, g):
    ms = jnp.mean(x * x, axis=-1, keepdims=True)
    return x * lax.rsqrt(ms + RMS_EPS) * g


def _log1p_exp_neg_abs(z):
    return jnp.log(1.0 + jnp.exp(-jnp.abs(z)))


def _ffn_kernel(*refs, final_norm, layer, n_f, tf):
    if final_norm:
        (x_ref, g_ref, wg_hbm, wu_hbm, wd_hbm, gf_ref, o_ref,
         n_ref, wg_buf, wu_buf, wd_buf, sem) = refs
    else:
        x_ref, g_ref, wg_hbm, wu_hbm, wd_hbm, o_ref, n_ref, wg_buf, wu_buf, wd_buf, sem = refs

    first = pl.program_id(0) * n_f
    total = pl.num_programs(0) * n_f

    def weight_copies(g):
        slot = g % FFN_RING
        cols = pl.ds(pl.multiple_of((g % n_f) * tf, tf), tf)
        return (pltpu.make_async_copy(wg_hbm.at[layer, :, cols], wg_buf.at[slot], sem.at[0, slot]),
                pltpu.make_async_copy(wu_hbm.at[layer, :, cols], wu_buf.at[slot], sem.at[1, slot]),
                pltpu.make_async_copy(wd_hbm.at[layer, cols, :], wd_buf.at[slot], sem.at[2, slot]))

    @pl.when(first == 0)
    def _():
        for g in range(FFN_AHEAD):
            for copy in weight_copies(g):
                copy.start()

    x = x_ref[...]
    n_ref[...] = _rms(x, g_ref[...]).astype(_BF16)
    o_ref[...] = x

    def step(j, carry):
        g = first + j
        slot = g % FFN_RING

        @pl.when(g + FFN_AHEAD < total)
        def _():
            for copy in weight_copies(g + FFN_AHEAD):
                copy.start()

        for copy in weight_copies(g):
            copy.wait()
        n = n_ref[...]
        gate = jnp.dot(n, wg_buf[slot].astype(_BF16), preferred_element_type=_F32)
        up = jnp.dot(n, wu_buf[slot].astype(_BF16), preferred_element_type=_F32)
        h = (gate * jax.nn.sigmoid(gate)) * (up * 0.5)
        o_ref[...] += jnp.dot(h.astype(_BF16), wd_buf[slot].astype(_BF16),
                              preferred_element_type=_F32)
        return carry

    lax.fori_loop(0, n_f, step, 0)
    if final_norm:
        o_ref[...] = _rms(o_ref[...], gf_ref[...])


def _ffn(x, g, wg, wu, wd, layer, final_g=None, *, tm=1024, tf=256):
    m, d = x.shape
    f = wg.shape[2]
    assert m % tm == 0 and f % tf == 0
    final_norm = final_g is not None
    hbm = pl.BlockSpec(memory_space=pl.ANY)
    in_specs = [pl.BlockSpec((tm, d), lambda i: (i, 0)), pl.BlockSpec((1, d), lambda i: (0, 0)),
                hbm, hbm, hbm]
    args = [x, g.reshape(1, d), wg, wu, wd]
    if final_norm:
        in_specs.append(pl.BlockSpec((1, d), lambda i: (0, 0)))
        args.append(final_g.reshape(1, d))
    est = (2 * (2 * tm * d * 4) + tm * d * 2 + FFN_RING * 3 * d * tf * 4 + 3 * d * tf * 2
           + 4 * tm * tf * 4)
    return pl.pallas_call(
        functools.partial(_ffn_kernel, final_norm=final_norm, layer=layer, n_f=f // tf, tf=tf),
        grid=(m // tm,),
        in_specs=in_specs,
        out_specs=pl.BlockSpec((tm, d), lambda i: (i, 0)),
        out_shape=jax.ShapeDtypeStruct((m, d), _F32),
        scratch_shapes=[pltpu.VMEM((tm, d), _BF16),
                        pltpu.VMEM((FFN_RING, d, tf), _F32), pltpu.VMEM((FFN_RING, d, tf), _F32),
                        pltpu.VMEM((FFN_RING, tf, d), _F32),
                        pltpu.SemaphoreType.DMA((3, FFN_RING))],
        compiler_params=pltpu.CompilerParams(
            dimension_semantics=("arbitrary",),
            vmem_limit_bytes=_vmem_limit(est)),
        name="ffn_final" if final_norm else "ffn",
    )(*args)


def _snake(i, j, n):
    return jnp.where(i % 2 == 0, j, n - 1 - j)


def _store_heads(o_ref, col, y):
    for c in range(0, y.shape[1], HEAD_DIM):
        o_ref[(col + c) // HEAD_DIM] = y[:, c:c + HEAD_DIM].astype(o_ref.dtype)


def _proj_gate_kernel(x_ref, g_ref, w_ref, wf_ref, o_ref, gate_ref, n_ref):
    j = pl.program_id(1)

    @pl.when(j == 0)
    def _():
        n = _rms(x_ref[...], g_ref[...]).astype(_BF16)
        n_ref[...] = n
        gate_ref[...] = jnp.dot(n, wf_ref[...], preferred_element_type=_F32)

    for c in range(0, w_ref.shape[1], PROJ_CHUNK):
        y = jnp.dot(n_ref[...], w_ref[:, c:c + PROJ_CHUNK].astype(_BF16),
                    preferred_element_type=_F32)
        _store_heads(o_ref, c, y)


def _proj_rope_kernel(x_ref, g_ref, wr_ref, wp_ref, cos_ref, sin_lo_ref, sin_hi_ref,
                      or_ref, op_ref, n_ref):
    @pl.when(pl.program_id(1) == 0)
    def _():
        n_ref[...] = _rms(x_ref[...], g_ref[...]).astype(_BF16)

    half = ROPE_DIMS // 2
    reps = PROJ_CHUNK // HEAD_DIM
    cos, sin_lo, sin_hi = (jnp.tile(t[...], (1, reps)) for t in (cos_ref, sin_lo_ref, sin_hi_ref))

    def project(w_ref, col):
        return jnp.dot(n_ref[...], w_ref[:, col:col + PROJ_CHUNK].astype(_BF16),
                       preferred_element_type=_F32)

    for col in range(0, wr_ref.shape[1], PROJ_CHUNK):
        y = project(wr_ref, col)
        _store_heads(or_ref, col, y * cos + pltpu.roll(y, PROJ_CHUNK - half, 1) * sin_lo
                     + pltpu.roll(y, half, 1) * sin_hi)
    for col in range(0, wp_ref.shape[1], PROJ_CHUNK):
        _store_heads(op_ref, col, project(wp_ref, col))


def _rope_tables(seq):
    half = ROPE_DIMS // 2
    freqs = ROPE_THETA ** (-jnp.arange(half, dtype=_F32) / half)
    ang = jnp.arange(seq, dtype=_F32)[:, None] * freqs[None, :]
    cos, sin = jnp.cos(ang), jnp.sin(ang)
    pad = jnp.zeros((seq, HEAD_DIM - ROPE_DIMS), _F32)
    zero = jnp.zeros((seq, half), _F32)
    cos_t = jnp.concatenate([cos, cos, pad + 1.0], axis=1)
    sin_lo = jnp.concatenate([-sin, zero, pad], axis=1)
    sin_hi = jnp.concatenate([zero, sin, pad], axis=1)
    return cos_t, sin_lo, sin_hi


def _proj_gate(x, g, w, layer, n_out, w_gate, *, tm=1024, tn=1024):
    m, d = x.shape
    assert m % tm == 0 and n_out % tn == 0 and n_out <= w.shape[2] and tn % PROJ_CHUNK == 0
    n_j = n_out // tn
    ng = w_gate.shape[1]
    est = (2 * tm * d * 4 + tm * d * 2 + 2 * d * tn * 4 + d * PROJ_CHUNK * 2 + 2 * tm * tn * 2
           + 2 * tm * PROJ_CHUNK * 4)
    return pl.pallas_call(
        _proj_gate_kernel,
        grid=(m // tm, n_j),
        in_specs=[pl.BlockSpec((tm, d), lambda i, j: (i, 0)),
                  pl.BlockSpec((1, d), lambda i, j: (0, 0)),
                  pl.BlockSpec((d, tn), lambda i, j: (layer, _snake(i, j, n_j))),
                  pl.BlockSpec((d, ng), lambda i, j: (0, 0))],
        out_specs=(pl.BlockSpec((tn // HEAD_DIM, tm, HEAD_DIM),
                                lambda i, j: (_snake(i, j, n_j), i, 0)),
                   pl.BlockSpec((tm, ng), lambda i, j: (i, 0))),
        out_shape=(jax.ShapeDtypeStruct((n_out // HEAD_DIM, m, HEAD_DIM), _BF16),
                   jax.ShapeDtypeStruct((m, ng), _F32)),
        scratch_shapes=[pltpu.VMEM((tm, d), _BF16)],
        compiler_params=pltpu.CompilerParams(
            dimension_semantics=("parallel", "arbitrary"),
            vmem_limit_bytes=_vmem_limit(est)),
        name="proj_gate",
    )(x, g.reshape(1, d), w.reshape(-1, w.shape[2]), w_gate)


def _proj_rope(x, g, w, layer, rope_cols, seq, *, tm=1024, n_j=4):
    m, d = x.shape
    n_out = w.shape[2]
    tr, tp = rope_cols // n_j, (n_out - rope_cols) // n_j
    assert m % tm == 0 and seq % tm == 0
    assert tr * n_j == rope_cols and tp * n_j == n_out - rope_cols
    assert tr % PROJ_CHUNK == 0 and tp % PROJ_CHUNK == 0 and rope_cols % tp == 0
    seq_blocks = seq // tm
    tab_spec = pl.BlockSpec((tm, HEAD_DIM), lambda i, j: (i % seq_blocks, 0))
    w2d = w.reshape(-1, n_out)
    est = (2 * tm * d * 4 + tm * d * 2 + 2 * d * (tr + tp) * 4 + d * PROJ_CHUNK * 2
           + 2 * tm * (tr + tp) * 2 + 2 * tm * PROJ_CHUNK * 4 + 3 * 2 * tm * HEAD_DIM * 4)
    return pl.pallas_call(
        _proj_rope_kernel,
        grid=(m // tm, n_j),
        in_specs=[pl.BlockSpec((tm, d), lambda i, j: (i, 0)),
                  pl.BlockSpec((1, d), lambda i, j: (0, 0)),
                  pl.BlockSpec((d, tr), lambda i, j: (layer, _snake(i, j, n_j))),
                  pl.BlockSpec((d, tp), lambda i, j: (layer, rope_cols // tp + _snake(i, j, n_j))),
                  tab_spec, tab_spec, tab_spec],
        out_specs=(pl.BlockSpec((tr // HEAD_DIM, tm, HEAD_DIM),
                                lambda i, j: (_snake(i, j, n_j), i, 0)),
                   pl.BlockSpec((tp // HEAD_DIM, tm, HEAD_DIM),
                                lambda i, j: (_snake(i, j, n_j), i, 0))),
        out_shape=(jax.ShapeDtypeStruct((rope_cols // HEAD_DIM, m, HEAD_DIM), _BF16),
                   jax.ShapeDtypeStruct(((n_out - rope_cols) // HEAD_DIM, m, HEAD_DIM), _BF16)),
        scratch_shapes=[pltpu.VMEM((tm, d), _BF16)],
        compiler_params=pltpu.CompilerParams(
            dimension_semantics=("parallel", "arbitrary"),
            vmem_limit_bytes=_vmem_limit(est)),
        name="proj_rope",
    )(x, g.reshape(1, d), w2d, w2d, *_rope_tables(seq))


def _split3(x):
    hi = x.astype(_BF16)
    r = x - hi.astype(_F32)
    mid = r.astype(_BF16)
    lo = (r - mid.astype(_F32)).astype(_BF16)
    return hi, mid, lo


def _gates_kernel(gp_ref, b_ref, col_ref, row_ref, *, blk):
    seq = gp_ref.shape[1]
    z = gp_ref[0] + b_ref[...]
    log_f = jnp.minimum(z, 0.0) - _log1p_exp_neg_abs(z)
    r = lax.broadcasted_iota(jnp.int32, (blk, blk), 0)
    c = lax.broadcasted_iota(jnp.int32, (blk, blk), 1)
    tri = jnp.where(c <= r, 1.0, 0.0).astype(_BF16)
    carry = jnp.zeros((1, log_f.shape[1]), _F32)
    for s in range(seq // blk):
        part = log_f[s * blk:(s + 1) * blk]
        cs = carry
        for p in _split3(part):
            cs = cs + jnp.dot(tri, p, preferred_element_type=_F32)
        col_ref[0, s * blk:(s + 1) * blk, :] = cs
        carry = cs[blk - 1:blk, :]
    row_ref[0] = jnp.transpose(col_ref[0])[:row_ref.shape[1], :]


def _gates(gate_pre, b_forget, batch, seq):
    ng = gate_pre.shape[1]
    nh = b_forget.shape[0]
    b_pad = jnp.zeros((1, ng), _F32).at[0, :nh].set(b_forget)
    return pl.pallas_call(
        functools.partial(_gates_kernel, blk=ATT_BLOCK),
        grid=(batch,),
        in_specs=[pl.BlockSpec((1, seq, ng), lambda b: (b, 0, 0)),
                  pl.BlockSpec((1, ng), lambda b: (0, 0))],
        out_specs=(pl.BlockSpec((1, seq, ng), lambda b: (b, 0, 0)),
                   pl.BlockSpec((1, nh, seq), lambda b: (b, 0, 0))),
        out_shape=(jax.ShapeDtypeStruct((batch, seq, ng), _F32),
                   jax.ShapeDtypeStruct((batch, nh, seq), _F32)),
        compiler_params=pltpu.CompilerParams(dimension_semantics=("parallel",)),
        name="gates",
    )(gate_pre.reshape(batch, seq, ng), b_pad)


def _block_iotas(blk):
    key = lax.broadcasted_iota(jnp.int32, (blk, blk), 0)
    qry = lax.broadcasted_iota(jnp.int32, (blk, blk), 1)
    return key, qry


def _rows(ref, j, blk):
    return ref[pl.ds(pl.multiple_of(j * blk, blk), blk), :]


def _cols(ref, j, blk):
    return ref[:, pl.ds(pl.multiple_of(j * blk, blk), blk)]


def _chain_sweep(logits, finish, states, chains, base, n_shared):
    def wave(w):
        return [(c, base + c - w, w, w == 0) for c in range(w, chains)]

    def shared(step):
        j = jnp.maximum(base - 1 - step, 0)
        return [(c, j, c + 1 + step, False) for c in range(chains)]

    states = list(states)
    us = logits(wave(0))
    for w in range(chains):
        if w + 1 < chains:
            ahead = logits(wave(w + 1))
        elif n_shared is not None:
            ahead = logits(shared(0))
        new = finish(wave(w), us, states[w:])
        states[w:] = new
        us = ahead
    if n_shared is None:
        return states

    def body(step, carry):
        us, sts = carry
        ahead = logits(shared(step + 1))
        return tuple(ahead), tuple(finish(shared(step), list(us), list(sts)))

    return lax.fori_loop(0, n_shared, body, (tuple(us), tuple(states)))[1]


def _sb_head(qt, k_ref, vt_ref, o_ref, tile, blk, chains, single_tile):
    scale2 = HEAD_DIM ** -0.5 * LOG2_E
    key, qry = _block_iotas(blk)
    strict = key < qry
    suffix = jnp.where(qry > key, 1.0, 0.0).astype(_BF16)

    def gates(z2, diagonal):
        log_beta = jnp.minimum(z2, 0.0) - jnp.log2(1.0 + jnp.exp2(-jnp.abs(z2)))
        log_not_beta = log_beta - z2
        if diagonal:
            log_not_beta = jnp.where(strict, log_not_beta, 0.0)
        hi = log_not_beta.astype(_BF16)
        lo = (log_not_beta - hi.astype(_F32)).astype(_BF16)
        return log_beta, log_not_beta, hi, lo

    def logits(items):
        return [jnp.dot(_rows(k_ref, j, blk), qt[c], preferred_element_type=_F32) * scale2
                for c, j, _, _ in items]

    def finish(items, z2, states):
        g = [gates(z, diag) for z, (_, _, _, diag) in zip(z2, items)]
        after = [jnp.dot(suffix, hi, preferred_element_type=_F32)
                 + jnp.dot(suffix, lo, preferred_element_type=_F32) + carry
                 for (_, _, hi, lo), (carry, _) in zip(g, states)]
        new = []
        for (log_beta, log_not_beta, _, _), aft, (carry, acc), (_, j, _, diag) in zip(
                g, after, states, items):
            w = jnp.exp2(log_beta + aft)
            if diag:
                w = jnp.where(strict, w, 0.0)
            acc = acc + jnp.dot(_cols(vt_ref, j, blk), w.astype(_BF16),
                                preferred_element_type=_F32)
            new.append((carry + jnp.sum(log_not_beta, axis=0, keepdims=True), acc))
        return new

    init = (jnp.zeros((1, blk), _F32), jnp.zeros((HEAD_DIM, blk), _F32))
    base = tile * chains
    states = _chain_sweep(logits, finish, [init] * chains, chains, base,
                          None if single_tile else base)
    for c in range(chains):
        o_ref[c * blk:(c + 1) * blk, :] = jnp.transpose(states[c][1].astype(o_ref.dtype))


def _softmax_finish(values, shifts=None):
    def finish(items, us, states):
        mids = []
        for u, (c, _, _, _), (m, l, _) in zip(us, items, states):
            u_max = jnp.max(u, axis=0, keepdims=True)
            shift = None if shifts is None else shifts[c]
            if shift is not None:
                u_max = u_max + shift
            m_new = jnp.maximum(m, u_max)
            alpha = jnp.exp2(m - m_new)
            p = jnp.exp2(u - m_new if shift is None else u + (shift - m_new))
            mids.append((m_new, alpha, alpha * l + jnp.sum(p, axis=0, keepdims=True),
                         p.astype(_BF16)))
        return [(m_new, l_new, alpha * acc + jnp.dot(values(j), p, preferred_element_type=_F32))
                for (m_new, alpha, l_new, p), (_, j, _, _), (_, _, acc) in zip(mids, items, states)]
    return finish


def _softmax_init(blk):
    return (jnp.full((1, blk), NEG_INF, _F32), jnp.zeros((1, blk), _F32),
            jnp.zeros((HEAD_DIM, blk), _F32))


def _softmax_store(o_ref, states, blk):
    for c, (_, l, acc) in enumerate(states):
        o_ref[c * blk:(c + 1) * blk, :] = jnp.transpose((acc / l).astype(o_ref.dtype))


FOX_BIAS_PARTS = 3


def _fox_augment_keys(k_ref, cfc_ref, hh, kaug_ref):
    seq = k_ref.shape[0]
    lane = lax.broadcasted_iota(jnp.int32, (seq, cfc_ref.shape[2]), 1)
    cf = jnp.sum(jnp.where(lane == hh, cfc_ref[0], 0.0), axis=1, keepdims=True)
    parts = _split3(cf * -(HEAD_DIM ** 0.5))
    lane = lax.broadcasted_iota(jnp.int32, (seq, HEAD_DIM), 1)
    extra = jnp.zeros((seq, HEAD_DIM), _F32)
    for i, part in enumerate(parts):
        extra = jnp.where(lane == i, part.astype(_F32), extra)
    kaug_ref[:, :HEAD_DIM] = k_ref[...]
    kaug_ref[:, HEAD_DIM:] = extra.astype(_BF16)


def _fox_head(qt, kaug_ref, vt_ref, cfr_ref, o_ref, hh, tile, blk, chains, single_tile):
    scale2 = HEAD_DIM ** -0.5 * LOG2_E
    key, qry = _block_iotas(blk)
    causal = key <= qry
    sub = lax.broadcasted_iota(jnp.int32, (HEAD_DIM, blk), 0)
    ones_rows = jnp.where(sub < FOX_BIAS_PARTS, 1.0, 0.0).astype(_BF16)
    base = tile * chains
    qt_aug = [jnp.concatenate([qt[c], ones_rows], axis=0) for c in range(chains)]
    cf_q = [cfr_ref[0, hh, pl.ds(base + c, 1), :] * LOG2_E for c in range(chains)]

    def logit(c, j, dist, diagonal):
        u = jnp.dot(_rows(kaug_ref, j, blk), qt_aug[c], preferred_element_type=_F32) * scale2
        return jnp.where(causal, u, NEG_INF) if diagonal else u

    states = _chain_sweep(lambda items: [logit(*it) for it in items],
                          _softmax_finish(lambda j: _cols(vt_ref, j, blk), cf_q),
                          [_softmax_init(blk)] * chains, chains, base,
                          None if single_tile else base)
    _softmax_store(o_ref, states, blk)


def _head_refs(gi, q_ref, k_ref, v_ref, o_ref):
    cols = pl.ds(pl.multiple_of(gi * HEAD_DIM, HEAD_DIM), HEAD_DIM)
    return q_ref.at[gi], k_ref.at[gi], v_ref.at[gi], o_ref.at[:, cols]


def _query_chains(q, blk, chains):
    qt_all = jnp.transpose(q[...])
    return [qt_all[:, c * blk:(c + 1) * blk] for c in range(chains)]


def _attn_even_kernel(q_ref, k_ref, v_ref, cfc_ref, cfr_ref, o_ref, vt_ref, kaug_ref,
                      *, blk, chains, single_tile, group):
    first_head = pl.program_id(1) * group
    tile = pl.program_id(2)

    def head(gi, stick_breaking):
        q, k, v, o = _head_refs(gi, q_ref, k_ref, v_ref, o_ref)
        vt_ref[...] = jnp.transpose(v[...])
        qt = _query_chains(q, blk, chains)
        if stick_breaking:
            _sb_head(qt, k, vt_ref, o, tile, blk, chains, single_tile)
        else:
            hh = first_head + gi - N_SB_HEADS
            _fox_augment_keys(k, cfc_ref, hh, kaug_ref)
            _fox_head(qt, kaug_ref, vt_ref, cfr_ref, o, hh, tile, blk, chains, single_tile)

    def heads(stick_breaking):
        def body(gi, carry):
            head(gi, stick_breaking)
            return carry
        lax.fori_loop(0, group, body, 0)

    pl.when(first_head < N_SB_HEADS)(functools.partial(heads, True))
    pl.when(first_head >= N_SB_HEADS)(functools.partial(heads, False))


def _attn_specs(batch, seq, n_heads, rows, group, v_first_head):
    assert n_heads % group == 0 and v_first_head % group == 0
    nq = seq // rows
    ng = n_heads // group
    v0 = v_first_head // group
    q_spec = pl.BlockSpec((group, rows, HEAD_DIM), lambda b, h, i: (h, b * nq + i, 0))
    k_spec = pl.BlockSpec((group, seq, HEAD_DIM), lambda b, h, i: (ng + h, b, 0))
    v_spec = pl.BlockSpec((group, seq, HEAD_DIM), lambda b, h, i: (v0 + h, b, 0))
    o_spec = pl.BlockSpec((rows, group * HEAD_DIM), lambda b, h, i: (b * nq + i, h))
    return nq, ng, q_spec, k_spec, v_spec, o_spec


def _attn_even(qkv, cf_col, cf_row, batch, seq, *, blk=ATT_BLOCK, chains=ATT_CHAINS,
               group=ATT_HEAD_GROUP):
    n_heads = N_SB_HEADS + N_FOX_HEADS
    assert N_SB_HEADS % group == 0
    nq, ng, q_spec, k_spec, v_spec, o_spec = _attn_specs(batch, seq, n_heads, blk * chains, group,
                                                         2 * n_heads)
    gate_lanes = cf_col.shape[2]
    nk = seq // blk
    cf_row = cf_row.reshape(batch, N_FOX_HEADS, nk, blk)
    return pl.pallas_call(
        functools.partial(_attn_even_kernel, blk=blk, chains=chains, single_tile=nq == 1,
                          group=group),
        grid=(batch, ng, nq),
        in_specs=[q_spec, k_spec, v_spec,
                  pl.BlockSpec((1, seq, gate_lanes), lambda b, h, i: (b, 0, 0)),
                  pl.BlockSpec((1, N_FOX_HEADS, nk, blk), lambda b, h, i: (b, 0, 0, 0))],
        out_specs=o_spec,
        out_shape=jax.ShapeDtypeStruct((batch * seq, n_heads * HEAD_DIM), _BF16),
        scratch_shapes=[pltpu.VMEM((HEAD_DIM, seq), _BF16),
                        pltpu.VMEM((seq, 2 * HEAD_DIM), _BF16)],
        compiler_params=pltpu.CompilerParams(
            dimension_semantics=("parallel", "parallel", "arbitrary")),
        name="attn_even",
    )(qkv, qkv, qkv, cf_col, cf_row)


def _dilated_bias_tables(blk):
    n_near = (DIL_W2 + blk - 1) // blk + 1
    assert blk % DIL_D3 == 0 and n_near * blk - (blk - 1) > DIL_W2
    d = jnp.arange(n_near + 1, dtype=jnp.int32)[:, None, None]
    s = jnp.arange(blk, dtype=jnp.int32)[None, :, None]
    t = jnp.arange(blk, dtype=jnp.int32)[None, None, :]
    delta = d * blk + t - s
    count = ((delta <= DIL_W1).astype(_F32)
             + ((delta % DIL_D2 == 0) & (delta <= DIL_W2)).astype(_F32)
             + (delta % DIL_D3 == 0).astype(_F32))
    count = jnp.where(delta >= 0, count, 0.0)
    return jnp.where(count > 0, jnp.log2(jnp.maximum(count, 1.0)), NEG_INF), n_near


def _attn_dilated_kernel(q_ref, k_ref, v_ref, bias_ref, o_ref, vt_ref, *, blk, chains,
                         n_near, single_tile, group):
    tile = pl.program_id(2)
    scale2 = HEAD_DIM ** -0.5 * LOG2_E
    base = tile * chains

    def head(gi, carry):
        q, k, v, o = _head_refs(gi, q_ref, k_ref, v_ref, o_ref)
        vt_ref[...] = jnp.transpose(v[...])
        qt = _query_chains(q, blk, chains)

        def logit(c, j, dist, diagonal):
            table = min(dist, n_near) if isinstance(dist, int) else jnp.minimum(dist, n_near)
            return (jnp.dot(_rows(k, j, blk), qt[c], preferred_element_type=_F32) * scale2
                    + bias_ref[table])

        states = _chain_sweep(lambda items: [logit(*it) for it in items],
                              _softmax_finish(lambda j: _cols(vt_ref, j, blk)),
                              [_softmax_init(blk)] * chains, chains, base,
                              None if single_tile else base)
        _softmax_store(o, states, blk)
        return carry

    lax.fori_loop(0, group, head, 0)


def _attn_dilated(qk, v, batch, seq, n_heads, *, blk=ATT_BLOCK, chains=ATT_CHAINS,
                  group=ATT_HEAD_GROUP):
    assert seq <= DIL_D3 * DIL_W1
    nq, ng, q_spec, k_spec, v_spec, o_spec = _attn_specs(batch, seq, n_heads, blk * chains, group, 0)
    bias, n_near = _dilated_bias_tables(blk)
    return pl.pallas_call(
        functools.partial(_attn_dilated_kernel, blk=blk, chains=chains, n_near=n_near,
                          single_tile=nq == 1, group=group),
        grid=(batch, ng, nq),
        in_specs=[q_spec, k_spec, v_spec,
                  pl.BlockSpec(bias.shape, lambda b, h, i: (0, 0, 0))],
        out_specs=o_spec,
        out_shape=jax.ShapeDtypeStruct((batch * seq, n_heads * HEAD_DIM), _BF16),
        scratch_shapes=[pltpu.VMEM((HEAD_DIM, seq), _BF16)],
        compiler_params=pltpu.CompilerParams(
            dimension_semantics=("parallel", "parallel", "arbitrary")),
        name="attn_dilated",
    )(qk, qk, v, bias)


def _outproj_kernel(o_ref, w_hbm, x_ref, y_ref, w_bf, stage, sem, *, layer, chunk):
    d = y_ref.shape[1]
    chunks = range(0, d, chunk)

    def weight_copy(c):
        slot = (c // chunk) % 2
        return pltpu.make_async_copy(w_hbm.at[layer, :, c:c + chunk], stage.at[slot], sem.at[slot])

    @pl.when(pl.program_id(0) == 0)
    def _():
        weight_copy(0).start()
        for c in chunks:
            if c + chunk < d:
                weight_copy(c + chunk).start()
            weight_copy(c).wait()
            w_bf[:, c:c + chunk] = stage[(c // chunk) % 2].astype(_BF16)

    for c in chunks:
        y_ref[:, c:c + chunk] = x_ref[:, c:c + chunk] + jnp.dot(
            o_ref[...], w_bf[:, c:c + chunk], preferred_element_type=_F32)


def _outproj(o, w, layer, x, *, tm=512, chunk=512):
    m, d_in = o.shape
    d = w.shape[2]
    assert m % tm == 0 and d % chunk == 0
    est = (2 * tm * d_in * 2 + 4 * tm * d * 4 + d_in * d * 2 + 2 * d_in * chunk * 4
           + tm * chunk * 4)
    return pl.pallas_call(
        functools.partial(_outproj_kernel, layer=layer, chunk=chunk),
        grid=(m // tm,),
        in_specs=[pl.BlockSpec((tm, d_in), lambda i: (i, 0)),
                  pl.BlockSpec(memory_space=pl.ANY),
                  pl.BlockSpec((tm, d), lambda i: (i, 0))],
        out_specs=pl.BlockSpec((tm, d), lambda i: (i, 0)),
        out_shape=jax.ShapeDtypeStruct((m, d), _F32),
        scratch_shapes=[pltpu.VMEM((d_in, d), _BF16), pltpu.VMEM((2, d_in, chunk), _F32),
                        pltpu.SemaphoreType.DMA((2,))],
        compiler_params=pltpu.CompilerParams(
            dimension_semantics=("arbitrary",),
            vmem_limit_bytes=_vmem_limit(est)),
        name="outproj",
    )(o, w, x)


def kernel(x, norm_g, ffn1_w_gate, ffn1_w_up, ffn1_w_down, ffn2_w_gate, ffn2_w_up,
           ffn2_w_down, even_w_in, even_b_forget, even_w_out, odd_w_qkv, odd_w_out,
           final_norm_g):
    batch, seq, d = x.shape
    depth = norm_g.shape[0]
    n_heads = d // HEAD_DIM
    assert n_heads == N_SB_HEADS + N_FOX_HEADS
    xf = x.reshape(batch * seq, d)
    for layer in range(depth):
        j = layer // 2
        xf = _ffn(xf, norm_g[layer, 0], ffn1_w_gate, ffn1_w_up, ffn1_w_down, layer)
        if layer % 2 == 0:
            w_gate = jnp.zeros((d, V7X_LANES), _BF16).at[:, :N_FOX_HEADS].set(
                even_w_in[j, :, 3 * d:].astype(_BF16))
            qkv, gate_pre = _proj_gate(xf, norm_g[layer, 1], even_w_in, j, 3 * d, w_gate)
            cf_col, cf_row = _gates(gate_pre, even_b_forget[j], batch, seq)
            o = _attn_even(qkv, cf_col, cf_row, batch, seq)
            xf = _outproj(o, even_w_out, j, xf)
        else:
            qk, v = _proj_rope(xf, norm_g[layer, 1], odd_w_qkv, j, 2 * d, seq)
            o = _attn_dilated(qk, v, batch, seq, n_heads)
            xf = _outproj(o, odd_w_out, j, xf)
        last = layer == depth - 1
        xf = _ffn(xf, norm_g[layer, 2], ffn2_w_gate, ffn2_w_up, ffn2_w_down, layer,
                  final_norm_g if last else None)
    return xf.reshape(batch, seq, d)
```
